```python
import math, functools
import jax, jax.numpy as jnp
from jax import lax
import numpy as np

D_MODEL = 2048
BATCH = 4
SEQ = 4096
DEPTH = 1
DEC_BATCH = 32
DEC_SEQ = 1
PAST_LEN = 16384
PAGE_SIZE = 128

PLE_DIM = 256
A_HEADS = 8
A_QK = 64
A_V = 2 * A_QK
A_QK_WIDTH = A_HEADS * 2 * A_QK
A_WIDTH = A_HEADS * A_V
R_HEAD = 64
R_WIDTH = D_MODEL // 2
R_HEADS = R_WIDTH // R_HEAD
R_DECAY_LORA = 64
R_ICLR_LORA = 64
SHIFT_W = 3 * R_WIDTH + R_DECAY_LORA + R_ICLR_LORA
IN_W = SHIFT_W + R_WIDTH + 2 * A_QK_WIDTH + 2 * A_WIDTH + 2 * D_MODEL
Q_BLOCK = 128
RMS_EPS = 1e-5
GN_EPS = 64e-5
NEG_INF = -1e30

kernel_name = 'hybrid_diffattn_rwkv7_step'


def _rms(x, g, eps=RMS_EPS):
    xf = x.astype(jnp.float32)
    y = xf * lax.rsqrt(jnp.mean(xf * xf, axis=-1, keepdims=True) + eps)
    return (y * g.astype(jnp.float32)).astype(x.dtype)


def _lambda_init(layer):
    return 0.8 - 0.6 * math.exp(-0.3 * layer)


def _alibi_slopes(n_heads):
    return 2.0 ** (-8.0 * jnp.arange(1, n_heads + 1, dtype=jnp.float32) / n_heads)


def _diff_block(q, k, v, qpos, kpos, slopes):
    s = jnp.einsum('bqhmd,bkhmd->bmhqk', q, k)
    dist = qpos[:, None] - kpos[None, :]
    s = jnp.where(dist >= 0, s - slopes[:, None, None] * dist.astype(jnp.float32), NEG_INF)
    m = jnp.max(s, axis=-1)
    pr = jnp.exp(s - m[..., None])
    return m, jnp.sum(pr, axis=-1), jnp.einsum('bmhqk,bkhe->bmhqe', pr, v)


def _combine(m, l, o):
    m_all = jnp.max(m, axis=0)
    w = jnp.exp(m - m_all)
    return jnp.sum(o * w[..., None], axis=0) / jnp.sum(l * w, axis=0)[..., None]


def _attn_prompt(q, k, v, slopes):
    b, s = q.shape[0], q.shape[1]
    nq = s // Q_BLOCK
    qb = jnp.swapaxes(q.reshape(b, nq, Q_BLOCK, A_HEADS, 2, A_QK), 0, 1)
    kpos = jnp.arange(s)

    def one(args):
        qi, start = args
        m, l, o = _diff_block(qi, k, v, start + jnp.arange(Q_BLOCK), kpos, slopes)
        return o / l[..., None]

    o = lax.map(one, (qb, jnp.arange(nq) * Q_BLOCK))
    return jnp.transpose(o, (1, 2, 3, 0, 4, 5)).reshape(b, 2, A_HEADS, s, A_V)


def _attn_sample(q, k, v, slopes, ck, cv, page_table):
    b, t = q.shape[0], q.shape[1]
    n_pages = page_table.shape[1]
    qpos = n_pages * PAGE_SIZE + jnp.arange(t)

    def page(args):
        phys, j = args
        kp = ck[phys].astype(jnp.float32).reshape(b, PAGE_SIZE, A_HEADS, 2, A_QK)
        vp = cv[phys].astype(jnp.float32)
        return _diff_block(q, kp, vp, qpos, j * PAGE_SIZE + jnp.arange(PAGE_SIZE), slopes)

    mp, lp, op = lax.map(page, (page_table.T, jnp.arange(n_pages)))
    ms, ls, os_ = _diff_block(q, k, v, qpos, qpos, slopes)
    return _combine(jnp.concatenate([mp, ms[None]], 0),
                    jnp.concatenate([lp, ls[None]], 0),
                    jnp.concatenate([op, os_[None]], 0))


def _wkv7_scan(s0, r, decay, k, v, kk, a):
    def step(S, inp):
        r_t, w_t, k_t, v_t, kk_t, a_t = inp
        sa = jnp.einsum('bhvk,bhk->bhv', S, -kk_t)
        S = (S * w_t[:, :, None, :] + sa[..., None] * (kk_t * a_t)[:, :, None, :]
             + v_t[..., None] * k_t[:, :, None, :])
        return S, jnp.einsum('bhvk,bhk->bhv', S, r_t)

    xs = tuple(jnp.moveaxis(z.astype(jnp.float32), 1, 0) for z in (r, decay, k, v, kk, a))
    s_fin, o = lax.scan(step, s0, xs)
    return s_fin, jnp.moveaxis(o, 0, 1)


def _layer(x, p, shift0, wkv0, attn_fn, layer, lw):
    (norm_g, w_in, mu, w0, w2, a0, a2, k_k, k_a, r_k, lnx_g, lnx_b, w_r_up,
     lq1, lk1, lq2, lk2, subln_g, w_a_up, w_out, w_ple, w_ple_gate) = lw
    b, t, _ = x.shape
    f32 = jnp.float32
    xn = _rms(x, norm_g)
    proj = xn @ w_in
    cuts = np.cumsum([SHIFT_W, R_WIDTH, A_QK_WIDTH, A_QK_WIDTH, A_WIDTH, A_WIDTH]).tolist()
    sh, z_r, q, k, v, z_a, gate_logits = jnp.split(proj, cuts, axis=-1)

    prev = jnp.concatenate([shift0[:, None, :].astype(sh.dtype), sh[:, :-1]], axis=1)
    sm = (sh + mu * (prev - sh)).astype(f32)
    r, kr, vr, wl, al = jnp.split(
        sm, [R_WIDTH, 2 * R_WIDTH, 3 * R_WIDTH, 3 * R_WIDTH + R_DECAY_LORA], axis=-1)
    w_log = -jax.nn.softplus(-(w0.astype(f32) + jnp.tanh(wl) @ w2.astype(f32))) - 0.5
    decay = jnp.exp(-jnp.exp(w_log))
    a = jax.nn.sigmoid(a0.astype(f32) + al @ a2.astype(f32))
    heads = lambda z: z.reshape(b, t, R_HEADS, R_HEAD)
    kk = heads(kr * k_k.astype(f32))
    kk = kk / jnp.maximum(jnp.sqrt(jnp.sum(kk * kk, axis=-1, keepdims=True)), 1e-12)
    kr = kr * (1.0 + (a - 1.0) * k_a.astype(f32))
    wkv_fin, o_r = _wkv7_scan(wkv0.astype(f32), heads(r), heads(decay), heads(kr),
                              heads(vr), kk, heads(a))
    mean = jnp.mean(o_r, axis=-1, keepdims=True)
    var = jnp.mean(jnp.square(o_r - mean), axis=-1, keepdims=True)
    gn_g = lnx_g.astype(f32).reshape(R_HEADS, R_HEAD)
    gn_b = lnx_b.astype(f32).reshape(R_HEADS, R_HEAD)
    o_r = (o_r - mean) * lax.rsqrt(var + GN_EPS) * gn_g + gn_b
    o_r = o_r + jnp.sum(heads(r) * heads(kr) * r_k.astype(f32), axis=-1, keepdims=True) * heads(vr)
    y_r = (o_r.reshape(b, t, R_WIDTH).astype(x.dtype) * jax.nn.silu(z_r)) @ w_r_up

    qh = q.astype(f32).reshape(b, t, A_HEADS, 2, A_QK) * (A_QK ** -0.5)
    kh = k.astype(f32).reshape(b, t, A_HEADS, 2, A_QK)
    vh = v.astype(f32).reshape(b, t, A_HEADS, A_V)
    lam0 = _lambda_init(layer)
    lam = (jnp.exp(jnp.sum(lq1 * lk1).astype(f32)) - jnp.exp(jnp.sum(lq2 * lk2).astype(f32)) + lam0)
    o = attn_fn(qh, kh, vh)
    o = jnp.swapaxes(o[:, 0] - lam * o[:, 1], 1, 2)
    o = _rms(o, subln_g) * (1.0 - lam0)
    y_a = (o.reshape(b, t, A_WIDTH).astype(x.dtype) * jax.nn.silu(z_a)) @ w_a_up

    g_a, g_r = jnp.split(jax.nn.sigmoid(gate_logits), 2, axis=-1)
    h = x + (g_a * y_a + g_r * y_r) @ w_out
    h = h + (p.astype(x.dtype) @ w_ple) * jax.nn.sigmoid(h @ w_ple_gate)
    return (h, k.reshape(b, t, A_HEADS, 2 * A_QK), v.reshape(b, t, A_HEADS, A_V), wkv_fin, sh[:, -1])


def setup_inputs(seed: int = 0) -> dict:
    key = jax.random.key(seed)
    ks = iter(jax.random.split(key, 48))

    def nrm(shape, scale=1.0):
        return jax.random.normal(next(ks), shape, jnp.float32) * scale

    n_pages = PAST_LEN // PAGE_SIZE
    n_phys = (DEC_BATCH * n_pages * 5) // 4
    page_table = jax.random.permutation(next(ks), n_phys)[:DEC_BATCH * n_pages]
    page_table = page_table.reshape(DEC_BATCH, n_pages).astype(jnp.int32)
    return {
        'x_prompt': nrm((BATCH, SEQ, D_MODEL)),
        'x_sample': nrm((DEC_BATCH, DEC_SEQ, D_MODEL)),
        'cache_k': nrm((DEPTH, n_phys, PAGE_SIZE, A_HEADS, 2 * A_QK)),
        'cache_v': nrm((DEPTH, n_phys, PAGE_SIZE, A_HEADS, A_V)),
        'state_wkv': nrm((DEPTH, DEC_BATCH, R_HEADS, R_HEAD, R_HEAD), 0.1),
        'state_shift': nrm((DEPTH, DEC_BATCH, SHIFT_W)),
        'page_table': page_table,
        'p_prompt': nrm((DEPTH, BATCH, SEQ, PLE_DIM)),
        'p_sample': nrm((DEPTH, DEC_BATCH, DEC_SEQ, PLE_DIM)),
        'norm_in_g': 1.0 + nrm((DEPTH, D_MODEL), 0.02),
        'w_in': nrm((DEPTH, D_MODEL, IN_W), D_MODEL ** -0.5),
        'mu_shift': jax.random.uniform(next(ks), (DEPTH, SHIFT_W), jnp.float32),
        'w0': jnp.linspace(-4.0, 0.0, R_WIDTH, dtype=jnp.float32)[None] + nrm((DEPTH, R_WIDTH), 0.1),
        'w2': nrm((DEPTH, R_DECAY_LORA, R_WIDTH), 0.1 * R_DECAY_LORA ** -0.5),
        'a0': nrm((DEPTH, R_WIDTH), 0.1),
        'a2': nrm((DEPTH, R_ICLR_LORA, R_WIDTH), R_ICLR_LORA ** -0.5),
        'k_k': 0.85 + nrm((DEPTH, R_WIDTH), 0.02),
        'k_a': 1.0 + nrm((DEPTH, R_WIDTH), 0.02),
        'r_k': nrm((DEPTH, R_HEADS, R_HEAD), 0.1),
        'lnx_g': 1.0 + nrm((DEPTH, R_WIDTH), 0.02),
        'lnx_b': nrm((DEPTH, R_WIDTH), 0.02),
        'w_rwkv_up': nrm((DEPTH, R_WIDTH, D_MODEL), R_WIDTH ** -0.5),
        'lambda_q1': nrm((DEPTH, A_QK), 0.1),
        'lambda_k1': nrm((DEPTH, A_QK), 0.1),
        'lambda_q2': nrm((DEPTH, A_QK), 0.1),
        'lambda_k2': nrm((DEPTH, A_QK), 0.1),
        'subln_g': 1.0 + nrm((DEPTH, A_V), 0.02),
        'w_attn_up': nrm((DEPTH, A_WIDTH, D_MODEL), A_WIDTH ** -0.5),
        'w_out': nrm((DEPTH, D_MODEL, D_MODEL), D_MODEL ** -0.5),
        'w_ple': nrm((DEPTH, PLE_DIM, D_MODEL), PLE_DIM ** -0.5),
        'w_ple_gate': nrm((DEPTH, D_MODEL, D_MODEL), D_MODEL ** -0.5),
        'norm_final_g': 1.0 + nrm((D_MODEL,), 0.02),
    }


def reference(x_prompt, x_sample, cache_k, cache_v, state_wkv, state_shift, page_table,
              p_prompt, p_sample, norm_in_g, w_in, mu_shift, w0, w2, a0, a2, k_k, k_a, r_k,
              lnx_g, lnx_b, w_rwkv_up, lambda_q1, lambda_k1, lambda_q2, lambda_k2, subln_g,
              w_attn_up, w_out, w_ple, w_ple_gate, norm_final_g):
    slopes = _alibi_slopes(A_HEADS)
    hp, hs = x_prompt, x_sample
    bp = x_prompt.shape[0]
    kp_l, vp_l, ks_l, vs_l, sp_l, ss_l, shp_l, shs_l = [], [], [], [], [], [], [], []
    for i in range(DEPTH):
        lw = (norm_in_g[i], w_in[i], mu_shift[i], w0[i], w2[i], a0[i], a2[i], k_k[i], k_a[i],
              r_k[i], lnx_g[i], lnx_b[i], w_rwkv_up[i], lambda_q1[i], lambda_k1[i],
              lambda_q2[i], lambda_k2[i], subln_g[i], w_attn_up[i], w_out[i], w_ple[i],
              w_ple_gate[i])
        attn_p = functools.partial(_attn_prompt, slopes=slopes)
        attn_s = functools.partial(_attn_sample, slopes=slopes, ck=cache_k[i], cv=cache_v[i],
                                   page_table=page_table)
        hp, kp, vp, sp, shp = _layer(hp, p_prompt[i], jnp.zeros((bp, SHIFT_W), hp.dtype),
                                     jnp.zeros((bp, R_HEADS, R_HEAD, R_HEAD), jnp.float32),
                                     attn_p, i, lw)
        hs, ksm, vsm, ss, shs = _layer(hs, p_sample[i], state_shift[i], state_wkv[i], attn_s, i, lw)
        kp_l.append(kp); vp_l.append(vp); ks_l.append(ksm); vs_l.append(vsm)
        sp_l.append(sp); ss_l.append(ss); shp_l.append(shp); shs_l.append(shs)
    y_prompt = _rms(hp, norm_final_g)
    y_sample = _rms(hs, norm_final_g)
    return (y_prompt, y_sample, jnp.stack(kp_l), jnp.stack(vp_l), jnp.stack(ks_l), jnp.stack(vs_l),
            jnp.stack(sp_l), jnp.stack(ss_l), jnp.stack(shp_l), jnp.stack(shs_l))
```

```python
import functools
import math

import jax
import jax.numpy as jnp
from jax import lax
from jax.experimental import pallas as pl
from jax.experimental.pallas import tpu as pltpu

F32 = jnp.float32
BF16 = jnp.bfloat16

LANES = 128
A_HEADS = 8
A_QK = 64
A_V = 2 * A_QK
R_HEAD = 64
RMS_EPS = 1e-5
GN_EPS = 64e-5
NEG_INF = -1e30
RWKV_CHUNK = 64
VMEM_LIMIT = 56 * 1024 * 1024

NT_DIMS = (((1,), (1,)), ((), ()))
TN_DIMS = (((0,), (0,)), ((), ()))


def _params(*sem):
    return pltpu.CompilerParams(dimension_semantics=sem, vmem_limit_bytes=VMEM_LIMIT)


def _dot(a, b):
    return jnp.dot(a, b, preferred_element_type=F32)


def _dg(a, b, dims):
    return lax.dot_general(a, b, dims, preferred_element_type=F32)


def _split_dot_rhs(a_bf16, x):
    hi = x.astype(BF16)
    lo = (x - hi.astype(F32)).astype(BF16)
    return _dot(a_bf16, hi) + _dot(a_bf16, lo)


def _split_dot_lhs(x, b_bf16):
    hi = x.astype(BF16)
    lo = (x - hi.astype(F32)).astype(BF16)
    return _dot(hi, b_bf16) + _dot(lo, b_bf16)


def _sigmoid(x):
    return 1.0 / (1.0 + jnp.exp(-x))


def _silu(x):
    return x * _sigmoid(x)


def _head_group_matrix():
    i = lax.broadcasted_iota(jnp.int32, (LANES, LANES), 0)
    j = lax.broadcasted_iota(jnp.int32, (LANES, LANES), 1)
    return ((i // R_HEAD) == (j // R_HEAD)).astype(BF16)


def _head_sums(x, gmat):
    cols = x.shape[1] // LANES
    parts = [_split_dot_lhs(x[:, c * LANES:(c + 1) * LANES], gmat) for c in range(cols)]
    return jnp.concatenate(parts, axis=1)


def _rms_kernel(x_ref, g_ref, o_ref):
    x = x_ref[...]
    y = x * lax.rsqrt(jnp.mean(x * x, axis=-1, keepdims=True) + RMS_EPS)
    o_ref[...] = (y * g_ref[...]).astype(o_ref.dtype)


def _rms_cast(x, g, tm):
    m, d = x.shape
    return pl.pallas_call(
        _rms_kernel,
        out_shape=jax.ShapeDtypeStruct((m, d), BF16),
        grid=(m // tm,),
        in_specs=[pl.BlockSpec((tm, d), lambda i: (i, 0)),
                  pl.BlockSpec((1, d), lambda i: (0, 0))],
        out_specs=pl.BlockSpec((tm, d), lambda i: (i, 0)),
        compiler_params=_params("parallel"),
        name="rms_cast",
    )(x, g.reshape(1, d))


def _mm_kernel(a_ref, b_ref, o_ref, *, scale):
    acc = _dot(a_ref[...], b_ref[...])
    if scale != 1.0:
        acc = acc * scale
    o_ref[...] = acc.astype(o_ref.dtype)


def _matmul(a, b, out_dtype, tm, tn, scale=1.0, name="matmul"):
    m, k = a.shape
    n = b.shape[1]
    return pl.pallas_call(
        functools.partial(_mm_kernel, scale=scale),
        out_shape=jax.ShapeDtypeStruct((m, n), out_dtype),
        grid=(m // tm, n // tn),
        in_specs=[pl.BlockSpec((tm, k), lambda i, j: (i, 0)),
                  pl.BlockSpec((k, tn), lambda i, j: (0, j))],
        out_specs=pl.BlockSpec((tm, tn), lambda i, j: (i, j)),
        compiler_params=_params("parallel", "parallel"),
        name=name,
    )(a, b)


def _rwkv_prep_math(sh, prev, mu, w0, w2, a0, a2, k_k, k_a):
    rw = w0.shape[1]
    sm = sh + mu * (prev - sh)
    r = sm[:, 0:rw]
    kr = sm[:, rw:2 * rw]
    vr = sm[:, 2 * rw:3 * rw]
    wl = sm[:, 3 * rw:3 * rw + w2.shape[0]]
    al = sm[:, 3 * rw + w2.shape[0]:]
    z = w0 + _dot(jnp.tanh(wl).astype(BF16), w2)
    softplus = jnp.maximum(-z, 0.0) + jnp.log1p(jnp.exp(-jnp.abs(z)))
    logw = -jnp.exp(-softplus - 0.5)
    a = _sigmoid(a0 + _dot(al.astype(BF16), a2))
    kk = kr * k_k
    gmat = _head_group_matrix()
    norm = jnp.maximum(jnp.sqrt(_head_sums(kk * kk, gmat)), 1e-12)
    kk = kk / norm
    kmod = kr * (1.0 + (a - 1.0) * k_a)
    return r, logw, kmod, vr, -kk, kk * a


def _rwkv_prep_prompt_kernel(sh_ref, tail_ref, mu_ref, w0_ref, w2_ref, a0_ref, a2_ref,
                             kk_ref, ka_ref, r_o, lw_o, k_o, v_o, al_o, be_o):
    sh = sh_ref[...]
    first = jnp.where(pl.program_id(1) == 0, 0.0, tail_ref[7:8, :])
    rolled = pltpu.roll(sh, 1, axis=0)
    row = lax.broadcasted_iota(jnp.int32, sh.shape, 0)
    prev = jnp.where(row == 0, first, rolled)
    outs = _rwkv_prep_math(sh, prev, mu_ref[...], w0_ref[...], w2_ref[...], a0_ref[...],
                           a2_ref[...], kk_ref[...], ka_ref[...])
    for ref, val in zip((r_o, lw_o, k_o, v_o, al_o, be_o), outs):
        ref[...] = val


def _rwkv_prep_sample_kernel(sh_ref, prev_ref, mu_ref, w0_ref, w2_ref, a0_ref, a2_ref,
                             kk_ref, ka_ref, r_o, lw_o, k_o, v_o, al_o, be_o):
    outs = _rwkv_prep_math(sh_ref[...], prev_ref[...], mu_ref[...], w0_ref[...], w2_ref[...],
                           a0_ref[...], a2_ref[...], kk_ref[...], ka_ref[...])
    for ref, val in zip((r_o, lw_o, k_o, v_o, al_o, be_o), outs):
        ref[...] = val


def _rwkv_prep(sh, prev, seq_len, tt, pw):
    rows, sw = sh.shape
    rw = pw["w0"].shape[1]
    const = lambda shape: pl.BlockSpec(shape, lambda *_: (0,) * len(shape))
    wspecs = [const((1, sw)), const((1, rw)), const(pw["w2"].shape), const((1, rw)),
              const(pw["a2"].shape), const((1, rw)), const((1, rw))]
    wargs = (pw["mu"], pw["w0"], pw["w2"], pw["a0"], pw["a2"], pw["k_k"], pw["k_a"])
    out_shape = [jax.ShapeDtypeStruct((rows, rw), F32)] * 6
    if prev is None:
        nt = seq_len // tt
        tile = lambda b, i: (b * nt + i, 0)
        tail = lambda b, i: (jnp.maximum((b * nt + i) * (tt // 8) - 1, 0), 0)
        return pl.pallas_call(
            _rwkv_prep_prompt_kernel, out_shape=out_shape, grid=(rows // seq_len, nt),
            in_specs=[pl.BlockSpec((tt, sw), tile), pl.BlockSpec((8, sw), tail)] + wspecs,
            out_specs=[pl.BlockSpec((tt, rw), tile)] * 6,
            compiler_params=_params("parallel", "parallel"), name="rwkv_prep_prompt",
        )(sh, sh, *wargs)
    return pl.pallas_call(
        _rwkv_prep_sample_kernel, out_shape=out_shape, grid=(1,),
        in_specs=[const((rows, sw)), const((rows, sw))] + wspecs,
        out_specs=[const((rows, rw))] * 6,
        compiler_params=_params("arbitrary"), name="rwkv_prep_sample",
    )(sh, prev, *wargs)


def _rwkv_scan_kernel(r_ref, lw_ref, k_ref, v_ref, al_ref, be_ref, o_ref, s_ref, s_scr, *, n_chunks):
    c = RWKV_CHUNK
    two_c = 2 * c

    @pl.when(pl.program_id(2) == 0)
    def _():
        s_scr[...] = jnp.zeros_like(s_scr)

    row = lax.broadcasted_iota(jnp.int32, (two_c, LANES), 0)
    col = lax.broadcasted_iota(jnp.int32, (two_c, LANES), 1)
    keep = (row >= c) == (col >= R_HEAD)
    strict = col < row
    incl = col <= row
    eye = (row == col).astype(F32)
    ti = lax.broadcasted_iota(jnp.int32, (c, c), 0)
    si = lax.broadcasted_iota(jnp.int32, (c, c), 1)
    ltri = (si <= ti).astype(BF16)

    def stack(x):
        return jnp.where(keep, jnp.concatenate([x, x], axis=0), 0.0).astype(BF16)

    def body(ci, carry):
        sl = pl.ds(pl.multiple_of(ci * c, c), c)
        r, lw, k, v = r_ref[sl, :], lw_ref[sl, :], k_ref[sl, :], v_ref[sl, :]
        al, be = al_ref[sl, :], be_ref[sl, :]
        cl = _split_dot_rhs(ltri, lw)
        e_pos = jnp.exp(cl)
        e_neg = jnp.exp(-cl)
        e_prev = jnp.exp(cl - lw)
        wc = e_pos[c - 1:c, :]
        kt = k * e_neg
        bt = be * e_neg
        la, lr = stack(al * e_prev), stack(r * e_pos)
        rb, rk, vs = stack(bt), stack(kt), stack(v)
        rbw, rkw = stack(bt * wc), stack(kt * wc)
        a_ab = jnp.where(strict, _dg(la, rb, NT_DIMS), 0.0)
        a_ak = jnp.where(strict, _dg(la, rk, NT_DIMS), 0.0).astype(BF16)
        a_rb = jnp.where(incl, _dg(lr, rb, NT_DIMS), 0.0).astype(BF16)
        a_rk = jnp.where(incl, _dg(lr, rk, NT_DIMS), 0.0).astype(BF16)
        p = eye + a_ab
        pw = a_ab
        for _ in range(int(math.log2(c)) - 1):
            pwb = pw.astype(BF16)
            pw = _dot(pwb, pwb)
            p = p + _dot(p.astype(BF16), pw.astype(BF16))
        tb = p.astype(BF16)
        ua = _dot(tb, la).astype(BF16)
        uv = _dot(tb, _dot(a_ak, vs).astype(BF16))
        s = s_scr[...]
        sb = s.astype(BF16)
        u = (_dg(ua, sb, NT_DIMS) + uv).astype(BF16)
        o = _dg(lr, sb, NT_DIMS) + _dot(a_rb, u) + _dot(a_rk, vs)
        o_ref[sl, :] = o[:c] + o[c:]
        upd = _dg(jnp.concatenate([u, vs], axis=0), jnp.concatenate([rbw, rkw], axis=0), TN_DIMS)
        s_scr[...] = s * wc + upd
        return carry

    lax.fori_loop(0, n_chunks, body, 0)
    s_ref[...] = s_scr[...]


def _rwkv_scan(r, lw, k, v, al, be, batch, seq_len, tt):
    rows, rw = r.shape
    pairs = rw // LANES
    nt = seq_len // tt
    tile = pl.BlockSpec((tt, LANES), lambda b, p, i: (b * nt + i, p))
    o, s = pl.pallas_call(
        functools.partial(_rwkv_scan_kernel, n_chunks=tt // RWKV_CHUNK),
        out_shape=[jax.ShapeDtypeStruct((rows, rw), F32),
                   jax.ShapeDtypeStruct((batch, pairs, LANES, LANES), F32)],
        grid=(batch, pairs, nt),
        in_specs=[tile] * 6,
        out_specs=[tile, pl.BlockSpec((None, None, LANES, LANES), lambda b, p, i: (b, p, 0, 0))],
        scratch_shapes=[pltpu.VMEM((LANES, LANES), F32)],
        compiler_params=_params("parallel", "parallel", "arbitrary"),
        name="rwkv_scan",
    )(r, lw, k, v, al, be)
    s = s.reshape(batch, pairs, 2, R_HEAD, 2, R_HEAD)
    s = jnp.stack([s[:, :, 0, :, 0, :], s[:, :, 1, :, 1, :]], axis=2)
    return o, s.reshape(batch, 2 * pairs, R_HEAD, R_HEAD)


def _rwkv_step_kernel(s_ref, r_ref, lw_ref, k_ref, al_ref, be_ref, vcol_ref, o_ref, so_ref):
    s = s_ref[...]
    sa = jnp.sum(s * al_ref[...], axis=-1, keepdims=True)
    s_new = s * jnp.exp(lw_ref[...]) + sa * be_ref[...] + vcol_ref[...] * k_ref[...]
    so_ref[...] = s_new
    o_ref[...] = jnp.sum(s_new * r_ref[...], axis=-1, keepdims=True)


def _rwkv_step(state, r, lw, k, v, al, be):
    b, h, n, _ = state.shape
    vec = lambda x: x.reshape(b, h, 1, n)
    vspec = pl.BlockSpec((None, h, 1, n), lambda i: (i, 0, 0, 0))
    cspec = pl.BlockSpec((None, h, n, 1), lambda i: (i, 0, 0, 0))
    sspec = pl.BlockSpec((None, h, n, n), lambda i: (i, 0, 0, 0))
    o, s_new = pl.pallas_call(
        _rwkv_step_kernel,
        out_shape=[jax.ShapeDtypeStruct((b, h, n, 1), F32), jax.ShapeDtypeStruct(state.shape, F32)],
        grid=(b,),
        in_specs=[sspec, vspec, vspec, vspec, vspec, vspec, cspec],
        out_specs=[cspec, sspec],
        compiler_params=_params("parallel"), name="rwkv_step",
    )(state, vec(r), vec(lw), vec(k), vec(al), vec(be), v.reshape(b, h, n, 1))
    return o.reshape(b, h * n), s_new


def _lambda(lq1, lk1, lq2, lk2, lam0):
    t1 = jnp.sum(lq1 * lk1, axis=-1, keepdims=True)
    t2 = jnp.sum(lq2 * lk2, axis=-1, keepdims=True)
    return jnp.exp(t1) - jnp.exp(t2) + lam0


def _diff_finish(o1, o2, lam, subln_g, lam0):
    o = o1 - lam * o2
    y = o * lax.rsqrt(jnp.mean(o * o, axis=-1, keepdims=True) + RMS_EPS)
    return y * subln_g * (1.0 - lam0)


def _attn_prompt_kernel(slopes_ref, q_ref, k_ref, v_ref, lq1, lk1, lq2, lk2, g_ref, o_ref,
                        kb, vb, m_scr, l_scr, acc_scr, *, tq, lam0):
    h = pl.program_id(1)
    qi = pl.program_id(2)

    @pl.when(qi == 0)
    def _():
        kb[...] = k_ref[...].astype(BF16)
        vb[...] = v_ref[...].astype(BF16)

    slope = slopes_ref[h]
    lane = lax.broadcasted_iota(jnp.int32, (tq, LANES), 1)
    q = q_ref[...]
    q2 = jnp.concatenate([jnp.where(lane < A_QK, q, 0), jnp.where(lane >= A_QK, q, 0)], axis=0)
    kcol = lax.broadcasted_iota(jnp.int32, (1, tq), 1).astype(F32) * slope
    m_scr[...] = jnp.full(m_scr.shape, NEG_INF, F32)
    l_scr[...] = jnp.zeros_like(l_scr)
    acc_scr[...] = jnp.zeros_like(acc_scr)

    def block(j, masked):
        ks = pl.ds(pl.multiple_of(j * tq, tq), tq)
        s = _dg(q2, kb[ks, :], NT_DIMS)
        s = s + (kcol - slope * ((qi - j) * tq).astype(F32))
        if masked:
            r_i = lax.broadcasted_iota(jnp.int32, (2 * tq, tq), 0) % tq
            c_i = lax.broadcasted_iota(jnp.int32, (2 * tq, tq), 1)
            s = jnp.where(c_i <= r_i, s, NEG_INF)
        m_old = m_scr[...]
        m_new = jnp.maximum(m_old, jnp.max(s, axis=-1, keepdims=True))
        p = jnp.exp(s - m_new)
        corr = jnp.exp(m_old - m_new)
        l_scr[...] = l_scr[...] * corr + jnp.sum(p, axis=-1, keepdims=True)
        acc_scr[...] = acc_scr[...] * corr + _dot(p.astype(BF16), vb[ks, :])
        m_scr[...] = m_new

    def loop_body(j, carry):
        block(j, False)
        return carry

    lax.fori_loop(0, qi, loop_body, 0)
    block(qi, True)

    o = acc_scr[...] / l_scr[...]
    lam = _lambda(lq1[...], lk1[...], lq2[...], lk2[...], lam0)
    o_ref[...] = _diff_finish(o[:tq], o[tq:], lam, g_ref[...], lam0)


def _attn_prompt(q, k, v, aw, batch, seq_len, tq, lam0):
    rows, width = k.shape
    nq = seq_len // tq
    qspec = pl.BlockSpec((tq, LANES), lambda b, h, i, *_: (b * nq + i, h))
    kvspec = pl.BlockSpec((seq_len, LANES), lambda b, h, i, *_: (b, h))
    vec = lambda n: pl.BlockSpec((1, n), lambda b, h, i, *_: (0, 0))
    grid_spec = pltpu.PrefetchScalarGridSpec(
        num_scalar_prefetch=1, grid=(batch, A_HEADS, nq),
        in_specs=[qspec, kvspec, kvspec, vec(A_QK), vec(A_QK), vec(A_QK), vec(A_QK), vec(A_V)],
        out_specs=qspec,
        scratch_shapes=[pltpu.VMEM((seq_len, LANES), BF16), pltpu.VMEM((seq_len, LANES), BF16),
                        pltpu.VMEM((2 * tq, 1), F32), pltpu.VMEM((2 * tq, 1), F32),
                        pltpu.VMEM((2 * tq, LANES), F32)])
    return pl.pallas_call(
        functools.partial(_attn_prompt_kernel, tq=tq, lam0=lam0),
        out_shape=jax.ShapeDtypeStruct((rows, width), F32),
        grid_spec=grid_spec,
        compiler_params=_params("parallel", "parallel", "arbitrary"),
        name="attn_prompt",
    )(aw["slopes"], q, k, v, aw["lq1"], aw["lk1"], aw["lq2"], aw["lk2"], aw["subln_g"])


def _attn_sample_kernel(pt_ref, qm_ref, slope_ref, ks_ref, vs_ref, lq1, lk1, lq2, lk2, g_ref, *rest,
                        ppb, n_pages, page, lam0):
    k_refs = rest[:ppb]
    v_refs = rest[ppb:2 * ppb]
    o_ref = rest[2 * ppb]
    m_scr, l_scr, acc_scr = rest[2 * ppb + 1:]
    j = pl.program_id(1)
    nrow = 2 * A_HEADS
    pcols = page * A_HEADS

    @pl.when(j == 0)
    def _():
        m_scr[...] = jnp.full(m_scr.shape, NEG_INF, F32)
        l_scr[...] = jnp.zeros_like(l_scr)
        acc_scr[...] = jnp.zeros_like(acc_scr)

    qm = qm_ref[...]
    slope = slope_ref[...]
    r_i = lax.broadcasted_iota(jnp.int32, (nrow, pcols), 0)
    c_i = lax.broadcasted_iota(jnp.int32, (nrow, pcols), 1)
    valid = (c_i % A_HEADS) == (r_i % A_HEADS)
    tok = (c_i // A_HEADS).astype(F32)
    for i in range(ppb):
        kp = k_refs[i][...].astype(BF16)
        vp = v_refs[i][...].astype(BF16)
        base = ((n_pages - (j * ppb + i)) * page).astype(F32)
        s = _dg(qm, kp, NT_DIMS) - slope * (base - tok)
        s = jnp.where(valid, s, NEG_INF)
        m_old = m_scr[...]
        m_new = jnp.maximum(m_old, jnp.max(s, axis=-1, keepdims=True))
        p = jnp.exp(s - m_new)
        corr = jnp.exp(m_old - m_new)
        l_scr[...] = l_scr[...] * corr + jnp.sum(p, axis=-1, keepdims=True)
        acc_scr[...] = acc_scr[...] * corr + _dot(p.astype(BF16), vp)
        m_scr[...] = m_new

    @pl.when(j == pl.num_programs(1) - 1)
    def _():
        k_self = jnp.concatenate([ks_ref[...], ks_ref[...]], axis=0).astype(BF16).astype(F32)
        v_self = jnp.concatenate([vs_ref[...], vs_ref[...]], axis=0).astype(BF16).astype(F32)
        s_self = jnp.sum(qm.astype(F32) * k_self, axis=-1, keepdims=True)
        m_old = m_scr[...]
        m_new = jnp.maximum(m_old, s_self)
        corr = jnp.exp(m_old - m_new)
        p_self = jnp.exp(s_self - m_new)
        l = l_scr[...] * corr + p_self
        o = (acc_scr[...] * corr + p_self * v_self) / l
        lam = _lambda(lq1[...], lk1[...], lq2[...], lk2[...], lam0)
        o_ref[...] = _diff_finish(o[:A_HEADS], o[A_HEADS:], lam, g_ref[...], lam0)


def _attn_sample(qm, k_self, v_self, cache_k, cache_v, page_table, aw, ppb, lam0):
    batch, n_pages = page_table.shape
    n_phys, page = cache_k.shape[0], cache_k.shape[1]
    ck = cache_k.reshape(n_phys, page * A_HEADS, LANES)
    cv = cache_v.reshape(n_phys, page * A_HEADS, LANES)
    nrow = 2 * A_HEADS

    def page_spec(i):
        return pl.BlockSpec((None, page * A_HEADS, LANES),
                            lambda b, j, pt: (pt[b, j * ppb + i], 0, 0))

    per_b = lambda r: pl.BlockSpec((None, r, LANES), lambda b, j, pt: (b, 0, 0))
    vec = lambda n: pl.BlockSpec((1, n), lambda b, j, pt: (0, 0))
    grid_spec = pltpu.PrefetchScalarGridSpec(
        num_scalar_prefetch=1, grid=(batch, n_pages // ppb),
        in_specs=[per_b(nrow), pl.BlockSpec((nrow, 1), lambda b, j, pt: (0, 0)),
                  per_b(A_HEADS), per_b(A_HEADS),
                  vec(A_QK), vec(A_QK), vec(A_QK), vec(A_QK), vec(A_V)]
                 + [page_spec(i) for i in range(ppb)] * 2,
        out_specs=per_b(A_HEADS),
        scratch_shapes=[pltpu.VMEM((nrow, 1), F32), pltpu.VMEM((nrow, 1), F32),
                        pltpu.VMEM((nrow, LANES), F32)])
    return pl.pallas_call(
        functools.partial(_attn_sample_kernel, ppb=ppb, n_pages=n_pages, page=page, lam0=lam0),
        out_shape=jax.ShapeDtypeStruct((batch, A_HEADS, LANES), F32),
        grid_spec=grid_spec,
        compiler_params=_params("parallel", "arbitrary"),
        name="attn_sample",
    )(page_table, qm, aw["slopes16"], k_self, v_self, aw["lq1"], aw["lk1"], aw["lq2"], aw["lk2"],
      aw["subln_g"], *([ck] * ppb), *([cv] * ppb))


def _merge_kernel(or_ref, r_ref, k_ref, v_ref, zr_ref, oa_ref, za_ref, ga_ref, gr_ref,
                  rk_ref, lng_ref, lnb_ref, wr_ref, wa_ref, o_ref):
    gmat = _head_group_matrix()
    o_r = or_ref[...]
    inv_n = 1.0 / R_HEAD
    mean = _head_sums(o_r, gmat) * inv_n
    d = o_r - mean
    var = _head_sums(d * d, gmat) * inv_n
    v = v_ref[...]
    bonus = _head_sums(r_ref[...] * k_ref[...] * rk_ref[...], gmat)
    o_r = d * lax.rsqrt(var + GN_EPS) * lng_ref[...] + lnb_ref[...] + bonus * v
    x_r = (o_r * _silu(zr_ref[...])).astype(BF16)
    x_a = (oa_ref[...] * _silu(za_ref[...])).astype(BF16)
    y_r = _dot(x_r, wr_ref[...])
    y_a = _dot(x_a, wa_ref[...])
    o_ref[...] = (_sigmoid(ga_ref[...]) * y_a + _sigmoid(gr_ref[...]) * y_r).astype(o_ref.dtype)


def _merge(o_r, r, k, v, o_a, zg, mw, tm):
    rows, rw = o_r.shape
    d = mw["w_r_up"].shape[1]
    tile = pl.BlockSpec((tm, rw), lambda i: (i, 0))
    const = lambda shape: pl.BlockSpec(shape, lambda i: (0,) * len(shape))
    return pl.pallas_call(
        _merge_kernel,
        out_shape=jax.ShapeDtypeStruct((rows, d), BF16),
        grid=(rows // tm,),
        in_specs=[tile, tile, tile, tile,
                  pl.BlockSpec((tm, rw), lambda i: (i, 0)),
                  tile,
                  pl.BlockSpec((tm, rw), lambda i: (i, 1)),
                  pl.BlockSpec((tm, d), lambda i: (i, 1)),
                  pl.BlockSpec((tm, d), lambda i: (i, 2)),
                  const((1, rw)), const((1, rw)), const((1, rw)),
                  const(mw["w_r_up"].shape), const(mw["w_a_up"].shape)],
        out_specs=pl.BlockSpec((tm, d), lambda i: (i, 0)),
        compiler_params=_params("parallel"), name="merge",
    )(o_r, r, k, v, zg, o_a, zg, zg, zg, mw["r_k"], mw["lnx_g"], mw["lnx_b"],
      mw["w_r_up"], mw["w_a_up"])


def _out_kernel(x_ref, m_ref, p_ref, wo_ref, wp_ref, wg_ref, g_ref, o_ref):
    h = x_ref[...] + _dot(m_ref[...], wo_ref[...])
    ple = _dot(p_ref[...].astype(BF16), wp_ref[...])
    h = h + ple * _sigmoid(_dot(h.astype(BF16), wg_ref[...]))
    y = h * lax.rsqrt(jnp.mean(h * h, axis=-1, keepdims=True) + RMS_EPS)
    o_ref[...] = y * g_ref[...]


def _out_stage(x, m, p, ow, tm):
    rows, d = x.shape
    pd = p.shape[1]
    const = lambda shape: pl.BlockSpec(shape, lambda i: (0,) * len(shape),
                                       pipeline_mode=pl.Buffered(1))
    return pl.pallas_call(
        _out_kernel,
        out_shape=jax.ShapeDtypeStruct((rows, d), F32),
        grid=(rows // tm,),
        in_specs=[pl.BlockSpec((tm, d), lambda i: (i, 0)), pl.BlockSpec((tm, d), lambda i: (i, 0)),
                  pl.BlockSpec((tm, pd), lambda i: (i, 0)),
                  const((d, d)), const((pd, d)), const((d, d)), const((1, d))],
        out_specs=pl.BlockSpec((tm, d), lambda i: (i, 0)),
        compiler_params=_params("parallel"), name="out_stage",
    )(x, m, p, ow["w_out"], ow["w_ple"], ow["w_ple_gate"], ow["norm_final_g"])


def _tile(n, pref):
    return pref if n % pref == 0 else n


def kernel(x_prompt, x_sample, cache_k, cache_v, state_wkv, state_shift, page_table, p_prompt,
           p_sample, norm_in_g, w_in, mu_shift, w0, w2, a0, a2, k_k, k_a, r_k, lnx_g, lnx_b,
           w_rwkv_up, lambda_q1, lambda_k1, lambda_q2, lambda_k2, subln_g, w_attn_up, w_out,
           w_ple, w_ple_gate, norm_final_g):
    depth = w_in.shape[0]
    assert depth == 1, "single-layer trunk"
    layer = 0
    lam0 = 0.8 - 0.6 * math.exp(-0.3 * layer)
    bp, tp, d = x_prompt.shape
    bs, ts, _ = x_sample.shape
    assert ts == 1
    rw = w0.shape[1]
    sw = mu_shift.shape[1]
    aqk = A_HEADS * 2 * A_QK
    aw_ = A_HEADS * A_V
    row = lambda x: x.reshape(1, -1)

    win = w_in[layer]
    cuts = [0, sw, sw + rw, sw + rw + aqk, sw + rw + 2 * aqk, sw + rw + 2 * aqk + aw_,
            sw + rw + 2 * aqk + 2 * aw_, win.shape[1]]
    seg = lambda i: win[:, cuts[i]:cuts[i + 1]]
    w_sh = seg(0).astype(BF16)
    w_q, w_k, w_v = seg(2).astype(BF16), seg(3).astype(BF16), seg(4).astype(BF16)
    w_zg = jnp.concatenate([seg(1), seg(5), seg(6)], axis=1).astype(BF16)

    pw = dict(mu=row(mu_shift[layer]), w0=row(w0[layer]), w2=w2[layer].astype(BF16),
              a0=row(a0[layer]), a2=a2[layer].astype(BF16), k_k=row(k_k[layer]), k_a=row(k_a[layer]))
    slopes = 2.0 ** (-8.0 * jnp.arange(1, A_HEADS + 1, dtype=F32) / A_HEADS)
    aw = dict(slopes=slopes, slopes16=jnp.tile(slopes, 2).reshape(2 * A_HEADS, 1),
              lq1=row(lambda_q1[layer]), lk1=row(lambda_k1[layer]), lq2=row(lambda_q2[layer]),
              lk2=row(lambda_k2[layer]), subln_g=row(subln_g[layer]))
    mw = dict(r_k=row(r_k[layer]), lnx_g=row(lnx_g[layer]), lnx_b=row(lnx_b[layer]),
              w_r_up=w_rwkv_up[layer].astype(BF16), w_a_up=w_attn_up[layer].astype(BF16))
    ow = dict(w_out=w_out[layer].astype(BF16), w_ple=w_ple[layer].astype(BF16),
              w_ple_gate=w_ple_gate[layer].astype(BF16), norm_final_g=row(norm_final_g))

    def project(x2d, tm):
        xn = _rms_cast(x2d, norm_in_g[layer], _tile(x2d.shape[0], 512))
        sh = _matmul(xn, w_sh, F32, tm, 640, name="proj_shift")
        q = _matmul(xn, w_q, BF16, tm, 1024, scale=A_QK ** -0.5, name="proj_q")
        k = _matmul(xn, w_k, F32, tm, 1024, name="proj_k")
        v = _matmul(xn, w_v, F32, tm, 1024, name="proj_v")
        zg = _matmul(xn, w_zg, F32, tm, 1024, name="proj_gates")
        return sh, q, k, v, zg

    xp = x_prompt.reshape(bp * tp, d)
    sh, q, k, v, zg = project(xp, _tile(bp * tp, 1024))
    r_, lw_, k_, v_, al_, be_ = _rwkv_prep(sh, None, tp, _tile(tp, 256), pw)
    o_r, wkv_p = _rwkv_scan(r_, lw_, k_, v_, al_, be_, bp, tp, _tile(tp, 1024))
    o_a = _attn_prompt(q, k, v, aw, bp, tp, _tile(tp, 512), lam0)
    m = _merge(o_r, r_, k_, v_, o_a, zg, mw, _tile(bp * tp, 256))
    y_p = _out_stage(xp, m, p_prompt[layer].reshape(bp * tp, -1), ow, _tile(bp * tp, 256))
    y_prompt = y_p.reshape(bp, tp, d)
    k_prompt = k.reshape(1, bp, tp, A_HEADS, 2 * A_QK)
    v_prompt = v.reshape(1, bp, tp, A_HEADS, A_V)
    shift_prompt = sh.reshape(bp, tp, sw)[:, -1][None]

    xs = x_sample.reshape(bs, d)
    sh_s, q_s, k_s, v_s, zg_s = project(xs, bs)
    r_, lw_, k_, v_, al_, be_ = _rwkv_prep(sh_s, state_shift[layer], 1, bs, pw)
    o_r, wkv_s = _rwkv_step(state_wkv[layer], r_, lw_, k_, v_, al_, be_)
    q_h = q_s.reshape(bs, 1, A_HEADS, 2, A_QK)
    sel = jnp.eye(2, dtype=q_s.dtype).reshape(1, 2, 1, 2, 1)
    qm = (q_h * sel).reshape(bs, 2 * A_HEADS, 2 * A_QK)
    o_a = _attn_sample(qm, k_s.reshape(bs, A_HEADS, 2 * A_QK), v_s.reshape(bs, A_HEADS, A_V),
                       cache_k[layer], cache_v[layer], page_table, aw,
                       4 if page_table.shape[1] % 4 == 0 else 1, lam0)
    m = _merge(o_r, r_, k_, v_, o_a.reshape(bs, aw_), zg_s, mw, bs)
    y_s = _out_stage(xs, m, p_sample[layer].reshape(bs, -1), ow, bs)
    y_sample = y_s.reshape(bs, ts, d)
    k_sample = k_s.reshape(1, bs, ts, A_HEADS, 2 * A_QK)
    v_sample = v_s.reshape(1, bs, ts, A_HEADS, A_V)

    return (y_prompt, y_sample, k_prompt, v_prompt, k_sample, v_sample,
            wkv_p[None], wkv_s[None], shift_prompt, sh_s[None])
```

```python
import functools
import math

import jax
import jax.numpy as jnp
from jax import lax
from jax.experimental import pallas as pl
from jax.experimental.pallas import tpu as pltpu

F32 = jnp.float32
BF16 = jnp.bfloat16

LANES = 128
A_HEADS = 8
A_QK = 64
A_V = 2 * A_QK
R_HEAD = 64
RMS_EPS = 1e-5
GN_EPS = 64e-5
NEG_INF = -1e30
RWKV_CHUNK = 64
VMEM_LIMIT = 56 * 1024 * 1024

NT_DIMS = (((1,), (1,)), ((), ()))
TN_DIMS = (((0,), (0,)), ((), ()))


def _params(*sem):
    return pltpu.CompilerParams(dimension_semantics=sem, vmem_limit_bytes=VMEM_LIMIT)


def _dot(a, b):
    return jnp.dot(a, b, preferred_element_type=F32)


def _dg(a, b, dims):
    return lax.dot_general(a, b, dims, preferred_element_type=F32)


def _split_dot_rhs(a_bf16, x):
    hi = x.astype(BF16)
    lo = (x - hi.astype(F32)).astype(BF16)
    return _dot(a_bf16, hi) + _dot(a_bf16, lo)


def _split_dot_lhs(x, b_bf16):
    hi = x.astype(BF16)
    lo = (x - hi.astype(F32)).astype(BF16)
    return _dot(hi, b_bf16) + _dot(lo, b_bf16)


def _sigmoid(x):
    return 1.0 / (1.0 + jnp.exp(-x))


def _silu(x):
    return x * _sigmoid(x)


def _head_group_matrix():
    i = lax.broadcasted_iota(jnp.int32, (LANES, LANES), 0)
    j = lax.broadcasted_iota(jnp.int32, (LANES, LANES), 1)
    return ((i // R_HEAD) == (j // R_HEAD)).astype(BF16)


def _head_sums(x, gmat):
    cols = x.shape[1] // LANES
    parts = [_split_dot_lhs(x[:, c * LANES:(c + 1) * LANES], gmat) for c in range(cols)]
    return jnp.concatenate(parts, axis=1)


def _rms_kernel(x_ref, g_ref, o_ref):
    x = x_ref[...]
    y = x * lax.rsqrt(jnp.mean(x * x, axis=-1, keepdims=True) + RMS_EPS)
    o_ref[...] = (y * g_ref[...]).astype(o_ref.dtype)


def _rms_cast(x, g, tm):
    m, d = x.shape
    return pl.pallas_call(
        _rms_kernel,
        out_shape=jax.ShapeDtypeStruct((m, d), BF16),
        grid=(m // tm,),
        in_specs=[pl.BlockSpec((tm, d), lambda i: (i, 0)),
                  pl.BlockSpec((1, d), lambda i: (0, 0))],
        out_specs=pl.BlockSpec((tm, d), lambda i: (i, 0)),
        compiler_params=_params("parallel"),
        name="rms_cast",
    )(x, g.reshape(1, d))


def _mm_kernel(a_ref, b_ref, o_ref, *, scale):
    acc = _dot(a_ref[...], b_ref[...])
    if scale != 1.0:
        acc = acc * scale
    o_ref[...] = acc.astype(o_ref.dtype)


def _matmul(a, b, out_dtype, tm, tn, scale=1.0, name="matmul"):
    m, k = a.shape
    n = b.shape[1]
    return pl.pallas_call(
        functools.partial(_mm_kernel, scale=scale),
        out_shape=jax.ShapeDtypeStruct((m, n), out_dtype),
        grid=(m // tm, n // tn),
        in_specs=[pl.BlockSpec((tm, k), lambda i, j: (i, 0)),
                  pl.BlockSpec((k, tn), lambda i, j: (0, j))],
        out_specs=pl.BlockSpec((tm, tn), lambda i, j: (i, j)),
        compiler_params=_params("parallel", "parallel"),
        name=name,
    )(a, b)


def _rwkv_prep_math(sh, prev, mu, w0, w2, a0, a2, k_k, k_a):
    rw = w0.shape[1]
    sm = sh + mu * (prev - sh)
    r = sm[:, 0:rw]
    kr = sm[:, rw:2 * rw]
    vr = sm[:, 2 * rw:3 * rw]
    wl = sm[:, 3 * rw:3 * rw + w2.shape[0]]
    al = sm[:, 3 * rw + w2.shape[0]:]
    z = w0 + _dot(jnp.tanh(wl).astype(BF16), w2)
    softplus = jnp.maximum(-z, 0.0) + jnp.log1p(jnp.exp(-jnp.abs(z)))
    logw = -jnp.exp(-softplus - 0.5)
    a = _sigmoid(a0 + _dot(al.astype(BF16), a2))
    kk = kr * k_k
    gmat = _head_group_matrix()
    norm = jnp.maximum(jnp.sqrt(_head_sums(kk * kk, gmat)), 1e-12)
    kk = kk / norm
    kmod = kr * (1.0 + (a - 1.0) * k_a)
    return r, logw, kmod, vr, -kk, kk * a


def _rwkv_prep_prompt_kernel(sh_ref, tail_ref, mu_ref, w0_ref, w2_ref, a0_ref, a2_ref,
                             kk_ref, ka_ref, r_o, lw_o, k_o, v_o, al_o, be_o):
    sh = sh_ref[...]
    first = jnp.where(pl.program_id(1) == 0, 0.0, tail_ref[7:8, :])
    rolled = pltpu.roll(sh, 1, axis=0)
    row = lax.broadcasted_iota(jnp.int32, sh.shape, 0)
    prev = jnp.where(row == 0, first, rolled)
    outs = _rwkv_prep_math(sh, prev, mu_ref[...], w0_ref[...], w2_ref[...], a0_ref[...],
                           a2_ref[...], kk_ref[...], ka_ref[...])
    for ref, val in zip((r_o, lw_o, k_o, v_o, al_o, be_o), outs):
        ref[...] = val


def _rwkv_prep_sample_kernel(sh_ref, prev_ref, mu_ref, w0_ref, w2_ref, a0_ref, a2_ref,
                             kk_ref, ka_ref, r_o, lw_o, k_o, v_o, al_o, be_o):
    outs = _rwkv_prep_math(sh_ref[...], prev_ref[...], mu_ref[...], w0_ref[...], w2_ref[...],
                           a0_ref[...], a2_ref[...], kk_ref[...], ka_ref[...])
    for ref, val in zip((r_o, lw_o, k_o, v_o, al_o, be_o), outs):
        ref[...] = val


def _rwkv_prep(sh, prev, seq_len, tt, pw):
    rows, sw = sh.shape
    rw = pw["w0"].shape[1]
    const = lambda shape: pl.BlockSpec(shape, lambda *_: (0,) * len(shape))
    wspecs = [const((1, sw)), const((1, rw)), const(pw["w2"].shape), const((1, rw)),
              const(pw["a2"].shape), const((1, rw)), const((1, rw))]
    wargs = (pw["mu"], pw["w0"], pw["w2"], pw["a0"], pw["a2"], pw["k_k"], pw["k_a"])
    out_shape = [jax.ShapeDtypeStruct((rows, rw), F32)] * 6
    if prev is None:
        nt = seq_len // tt
        tile = lambda b, i: (b * nt + i, 0)
        tail = lambda b, i: (jnp.maximum((b * nt + i) * (tt // 8) - 1, 0), 0)
        return pl.pallas_call(
            _rwkv_prep_prompt_kernel, out_shape=out_shape, grid=(rows // seq_len, nt),
            in_specs=[pl.BlockSpec((tt, sw), tile), pl.BlockSpec((8, sw), tail)] + wspecs,
            out_specs=[pl.BlockSpec((tt, rw), tile)] * 6,
            compiler_params=_params("parallel", "parallel"), name="rwkv_prep_prompt",
        )(sh, sh, *wargs)
    return pl.pallas_call(
        _rwkv_prep_sample_kernel, out_shape=out_shape, grid=(1,),
        in_specs=[const((rows, sw)), const((rows, sw))] + wspecs,
        out_specs=[const((rows, rw))] * 6,
        compiler_params=_params("arbitrary"), name="rwkv_prep_sample",
    )(sh, prev, *wargs)


def _rwkv_chunk_kernel(r_ref, lw_ref, k_ref, v_ref, al_ref, be_ref, m_o, n_o, lr_o, op_o, wc_o,
                       *, n_chunks):
    c = RWKV_CHUNK
    two_c = 2 * c
    row = lax.broadcasted_iota(jnp.int32, (two_c, LANES), 0)
    col = lax.broadcasted_iota(jnp.int32, (two_c, LANES), 1)
    keep = (row >= c) == (col >= R_HEAD)
    strict = col < row
    incl = col <= row
    eye = (row == col).astype(F32)
    ti = lax.broadcasted_iota(jnp.int32, (c, c), 0)
    si = lax.broadcasted_iota(jnp.int32, (c, c), 1)
    ltri = (si <= ti).astype(BF16)

    def stack(x):
        return jnp.where(keep, jnp.concatenate([x, x], axis=0), 0.0).astype(BF16)

    cs = range(n_chunks)
    each = lambda fn, *lists: [fn(*xs) for xs in zip(*lists)]
    load = lambda ref: [ref[pl.ds(ci * c, c), :] for ci in cs]
    r, lw, k, v, al, be = (load(ref) for ref in (r_ref, lw_ref, k_ref, v_ref, al_ref, be_ref))
    cl = each(lambda x: _split_dot_rhs(ltri, x), lw)
    e_pos = each(jnp.exp, cl)
    e_neg = each(lambda x: jnp.exp(-x), cl)
    e_prev = each(lambda x, y: jnp.exp(x - y), cl, lw)
    wc = each(lambda x: x[c - 1:c, :], e_pos)
    kt = each(jnp.multiply, k, e_neg)
    bt = each(jnp.multiply, be, e_neg)
    la = each(lambda x, y: stack(x * y), al, e_prev)
    lr = each(lambda x, y: stack(x * y), r, e_pos)
    rb, rk, vs = each(stack, bt), each(stack, kt), each(stack, v)
    rbw = each(lambda x, y: stack(x * y), bt, wc)
    rkw = each(lambda x, y: stack(x * y), kt, wc)
    nt_dot = lambda x, y: _dg(x, y, NT_DIMS)
    a_ab = each(lambda x, y: jnp.where(strict, nt_dot(x, y), 0.0), la, rb)
    a_ak = each(lambda x, y: jnp.where(strict, nt_dot(x, y), 0.0).astype(BF16), la, rk)
    a_rb = each(lambda x, y: jnp.where(incl, nt_dot(x, y), 0.0).astype(BF16), lr, rb)
    a_rk = each(lambda x, y: jnp.where(incl, nt_dot(x, y), 0.0).astype(BF16), lr, rk)
    p = each(lambda x: eye + x, a_ab)
    pw = a_ab
    for _ in range(int(math.log2(c)) - 1):
        pwb = each(lambda x: x.astype(BF16), pw)
        pw = each(_dot, pwb, pwb)
        p = each(lambda x, y: x + _dot(x.astype(BF16), y.astype(BF16)), p, pw)
    tb = each(lambda x: x.astype(BF16), p)
    ua = each(lambda x, y: _dot(x, y).astype(BF16), tb, la)
    akv = each(lambda x, y: _dot(x, y).astype(BF16), a_ak, vs)
    uv = each(lambda x, y: _dot(x, y).astype(BF16), tb, akv)
    lr2 = each(lambda x, y, z: (x.astype(F32) + _dot(y, z)).astype(BF16), lr, a_rb, ua)
    op = each(lambda x, y, z, w: _dot(x, y) + _dot(z, w), a_rb, uv, a_rk, vs)
    m2 = each(lambda x, y: _dg(x, y, TN_DIMS).astype(BF16), ua, rbw)
    n2 = each(lambda x, y, z, w: _dg(jnp.concatenate([x, y], axis=0),
                                     jnp.concatenate([z, w], axis=0), TN_DIMS), uv, vs, rbw, rkw)
    for ci in cs:
        lr_o[ci] = lr2[ci]
        op_o[pl.ds(ci * c, c), :] = op[ci][:c] + op[ci][c:]
        m_o[ci] = m2[ci]
        n_o[ci] = n2[ci]
        wc_o[ci] = wc[ci]


def _rwkv_state_kernel(m_ref, n_ref, lr_ref, op_ref, wc_ref, o_ref, s_ref, s_scr, *, n_chunks, pairs):
    c = RWKV_CHUNK

    @pl.when(pl.program_id(1) == 0)
    def _():
        s_scr[...] = jnp.zeros_like(s_scr)

    def body(ci, carry):
        sl = pl.ds(pl.multiple_of(ci * c, c), c)
        outs = []
        for p in range(pairs):
            s = s_scr[p]
            sb = s.astype(BF16)
            o = _dg(lr_ref[ci, p], sb, NT_DIMS)
            outs.append(o[:c] + o[c:])
            s_scr[p] = s * wc_ref[ci, p] + _dot(sb, m_ref[ci, p]) + n_ref[ci, p]
        o_ref[sl, :] = jnp.concatenate(outs, axis=1) + op_ref[sl, :]
        return carry

    lax.fori_loop(0, n_chunks, body, 0)
    s_ref[...] = s_scr[...]


def _rwkv_scan(r, lw, k, v, al, be, batch, seq_len, tt_chunk, tt_state):
    rows, rw = r.shape
    pairs = rw // LANES
    c = RWKV_CHUNK
    n_all = rows // c
    nb = tt_chunk // c
    tile = pl.BlockSpec((tt_chunk, LANES), lambda i, p: (i, p))
    mat = pl.BlockSpec((nb, None, LANES, LANES), lambda i, p: (i, p, 0, 0))
    m_, n_, lr_, op_, wc_ = pl.pallas_call(
        functools.partial(_rwkv_chunk_kernel, n_chunks=nb),
        out_shape=[jax.ShapeDtypeStruct((n_all, pairs, LANES, LANES), BF16),
                   jax.ShapeDtypeStruct((n_all, pairs, LANES, LANES), F32),
                   jax.ShapeDtypeStruct((n_all, pairs, 2 * c, LANES), BF16),
                   jax.ShapeDtypeStruct((rows, rw), F32),
                   jax.ShapeDtypeStruct((n_all, pairs, 1, LANES), F32)],
        grid=(rows // tt_chunk, pairs),
        in_specs=[tile] * 6,
        out_specs=[mat, mat, pl.BlockSpec((nb, None, 2 * c, LANES), lambda i, p: (i, p, 0, 0)), tile,
                   pl.BlockSpec((nb, None, 1, LANES), lambda i, p: (i, p, 0, 0))],
        compiler_params=_params("parallel", "parallel"),
        name="rwkv_chunk",
    )(r, lw, k, v, al, be)

    nt = seq_len // tt_state
    ns = tt_state // c
    smat = lambda rws: pl.BlockSpec((ns, pairs, rws, LANES), lambda b, i: (b * nt + i, 0, 0, 0))
    wide = pl.BlockSpec((tt_state, rw), lambda b, i: (b * nt + i, 0))
    o, s = pl.pallas_call(
        functools.partial(_rwkv_state_kernel, n_chunks=ns, pairs=pairs),
        out_shape=[jax.ShapeDtypeStruct((rows, rw), F32),
                   jax.ShapeDtypeStruct((batch, pairs, LANES, LANES), F32)],
        grid=(batch, nt),
        in_specs=[smat(LANES), smat(LANES), smat(2 * c), wide, smat(1)],
        out_specs=[wide, pl.BlockSpec((None, pairs, LANES, LANES), lambda b, i: (b, 0, 0, 0))],
        scratch_shapes=[pltpu.VMEM((pairs, LANES, LANES), F32)],
        compiler_params=_params("parallel", "arbitrary"),
        name="rwkv_state",
    )(m_, n_, lr_, op_, wc_)
    s = s.reshape(batch, pairs, 2, R_HEAD, 2, R_HEAD)
    s = jnp.stack([s[:, :, 0, :, 0, :], s[:, :, 1, :, 1, :]], axis=2)
    return o, s.reshape(batch, 2 * pairs, R_HEAD, R_HEAD)


def _rwkv_step_kernel(s_ref, r_ref, lw_ref, k_ref, al_ref, be_ref, vcol_ref, o_ref, so_ref):
    s = s_ref[...]
    sa = jnp.sum(s * al_ref[...], axis=-1, keepdims=True)
    s_new = s * jnp.exp(lw_ref[...]) + sa * be_ref[...] + vcol_ref[...] * k_ref[...]
    so_ref[...] = s_new
    o_ref[...] = jnp.sum(s_new * r_ref[...], axis=-1, keepdims=True)


def _rwkv_step(state, r, lw, k, v, al, be):
    b, h, n, _ = state.shape
    vec = lambda x: x.reshape(b, h, 1, n)
    vspec = pl.BlockSpec((None, h, 1, n), lambda i: (i, 0, 0, 0))
    cspec = pl.BlockSpec((None, h, n, 1), lambda i: (i, 0, 0, 0))
    sspec = pl.BlockSpec((None, h, n, n), lambda i: (i, 0, 0, 0))
    o, s_new = pl.pallas_call(
        _rwkv_step_kernel,
        out_shape=[jax.ShapeDtypeStruct((b, h, n, 1), F32), jax.ShapeDtypeStruct(state.shape, F32)],
        grid=(b,),
        in_specs=[sspec, vspec, vspec, vspec, vspec, vspec, cspec],
        out_specs=[cspec, sspec],
        compiler_params=_params("parallel"), name="rwkv_step",
    )(state, vec(r), vec(lw), vec(k), vec(al), vec(be), v.reshape(b, h, n, 1))
    return o.reshape(b, h * n), s_new


def _lambda(lq1, lk1, lq2, lk2, lam0):
    t1 = jnp.sum(lq1 * lk1, axis=-1, keepdims=True)
    t2 = jnp.sum(lq2 * lk2, axis=-1, keepdims=True)
    return jnp.exp(t1) - jnp.exp(t2) + lam0


def _diff_finish(o1, o2, lam, subln_g, lam0):
    o = o1 - lam * o2
    y = o * lax.rsqrt(jnp.mean(o * o, axis=-1, keepdims=True) + RMS_EPS)
    return y * subln_g * (1.0 - lam0)


ATTN_GROUP = 256
ALIBI_SPLIT = 256


def _attn_prompt_kernel(slopes_ref, q_ref, k_ref, v_ref, lq1, lk1, lq2, lk2, g_ref, o_ref,
                        kb, vt, *, tq, lam0):
    h = pl.program_id(1)
    qi = pl.program_id(2)
    seq_len = k_ref.shape[0]

    @pl.when(qi == 0)
    def _():
        slope = slopes_ref[h]
        pos = lax.broadcasted_iota(jnp.int32, (seq_len, LANES), 0)
        lane = lax.broadcasted_iota(jnp.int32, (seq_len, LANES), 1)
        lo = (pos % ALIBI_SPLIT).astype(F32) * slope
        hi = (pos - pos % ALIBI_SPLIT).astype(F32) * slope
        kb[:, :LANES] = k_ref[...].astype(BF16)
        kb[:, LANES:] = jnp.where(lane == 0, lo, jnp.where(lane == 1, hi, 0.0)).astype(BF16)
        for c0 in range(0, seq_len, tq):
            vt[:, c0:c0 + tq] = v_ref[c0:c0 + tq, :].T.astype(BF16)

    lane = lax.broadcasted_iota(jnp.int32, (tq, LANES), 1)
    q = q_ref[...]
    ones = jnp.where(lane < 2, 1.0, 0.0).astype(BF16)
    zero = jnp.zeros_like(q)
    q_aug = jnp.concatenate([
        jnp.concatenate([jnp.where(lane < A_QK, q, zero), ones], axis=1),
        jnp.concatenate([jnp.where(lane >= A_QK, q, zero), ones], axis=1)], axis=0)
    gw = ATTN_GROUP
    groups = range(2 * tq // gw)
    q_grp = [q_aug[g * gw:(g + 1) * gw, :] for g in groups]

    def block(j, carry, diagonal):
        m, l, acc = (list(x) for x in carry)
        ks = pl.ds(pl.multiple_of(j * tq, tq), tq)
        kbj = kb[ks, :]
        vtj = vt[:, ks]

        def scores(g):
            rows = min(tq, (g * gw) % tq + gw) if diagonal else tq
            s = _dg(kbj[:rows, :], q_grp[g], NT_DIMS)
            if diagonal:
                r_i = lax.broadcasted_iota(jnp.int32, (rows, gw), 0)
                c_i = lax.broadcasted_iota(jnp.int32, (rows, gw), 1) + (g * gw) % tq
                s = jnp.where(r_i <= c_i, s, NEG_INF)
            return s, rows

        nxt = scores(0)
        for g in groups:
            s, rows = nxt
            if g + 1 < len(groups):
                nxt = scores(g + 1)
            m_new = jnp.maximum(m[g], jnp.max(s, axis=0, keepdims=True))
            p = jnp.exp(s - m_new)
            corr = jnp.exp(m[g] - m_new)
            l[g] = l[g] * corr + jnp.sum(p, axis=0, keepdims=True)
            acc[g] = acc[g] * corr + _dot(vtj[:, :rows], p.astype(BF16))
            m[g] = m_new
        return tuple(m), tuple(l), tuple(acc)

    init = (tuple(jnp.full((1, gw), NEG_INF, F32) for _ in groups),
            tuple(jnp.zeros((1, gw), F32) for _ in groups),
            tuple(jnp.zeros((LANES, gw), F32) for _ in groups))
    carry = lax.fori_loop(0, qi, lambda j, cr: block(j, cr, False), init)
    _, l, acc = block(qi, carry, True)

    o = jnp.concatenate([a / x for a, x in zip(acc, l)], axis=1).T
    lam = _lambda(lq1[...], lk1[...], lq2[...], lk2[...], lam0)
    o_ref[...] = _diff_finish(o[:tq], o[tq:], lam, g_ref[...], lam0)


def _attn_prompt(q, k, v, aw, batch, seq_len, tq, lam0):
    rows, width = k.shape
    nq = seq_len // tq
    qspec = pl.BlockSpec((tq, LANES), lambda b, h, i, *_: (b * nq + i, h))
    kvspec = pl.BlockSpec((seq_len, LANES), lambda b, h, i, *_: (b, h))
    vec = lambda n: pl.BlockSpec((1, n), lambda b, h, i, *_: (0, 0))
    grid_spec = pltpu.PrefetchScalarGridSpec(
        num_scalar_prefetch=1, grid=(batch, A_HEADS, nq),
        in_specs=[qspec, kvspec, kvspec, vec(A_QK), vec(A_QK), vec(A_QK), vec(A_QK), vec(A_V)],
        out_specs=qspec,
        scratch_shapes=[pltpu.VMEM((seq_len, 2 * LANES), BF16), pltpu.VMEM((LANES, seq_len), BF16)])
    return pl.pallas_call(
        functools.partial(_attn_prompt_kernel, tq=tq, lam0=lam0),
        out_shape=jax.ShapeDtypeStruct((rows, width), F32),
        grid_spec=grid_spec,
        compiler_params=_params("parallel", "parallel", "arbitrary"),
        name="attn_prompt",
    )(aw["slopes"], q, k, v, aw["lq1"], aw["lk1"], aw["lq2"], aw["lk2"], aw["subln_g"])


def _attn_sample_kernel(pt_ref, qm_ref, slope_ref, ks_ref, vs_ref, lq1, lk1, lq2, lk2, g_ref, *rest,
                        ppb, n_pages, page, lam0):
    k_refs = rest[:ppb]
    v_refs = rest[ppb:2 * ppb]
    o_ref = rest[2 * ppb]
    m_scr, l_scr, acc_scr = rest[2 * ppb + 1:]
    j = pl.program_id(1)
    nrow = 2 * A_HEADS
    pcols = page * A_HEADS

    @pl.when(j == 0)
    def _():
        m_scr[...] = jnp.full(m_scr.shape, NEG_INF, F32)
        l_scr[...] = jnp.zeros_like(l_scr)
        acc_scr[...] = jnp.zeros_like(acc_scr)

    qm = qm_ref[...]
    slope = slope_ref[...]
    r_i = lax.broadcasted_iota(jnp.int32, (nrow, pcols), 0)
    c_i = lax.broadcasted_iota(jnp.int32, (nrow, pcols), 1)
    valid = (c_i % A_HEADS) == (r_i % A_HEADS)
    tok = (c_i // A_HEADS).astype(F32)
    scores = []
    for i in range(ppb):
        base = ((n_pages - (j * ppb + i)) * page).astype(F32)
        s = _dg(qm, k_refs[i][...].astype(BF16), NT_DIMS) - slope * (base - tok)
        scores.append(jnp.where(valid, s, NEG_INF))
    m_old = m_scr[...]
    m_blk = functools.reduce(jnp.maximum, [jnp.max(s, axis=-1, keepdims=True) for s in scores])
    m_new = jnp.maximum(m_old, m_blk)
    corr = jnp.exp(m_old - m_new)
    l_new = l_scr[...] * corr
    acc = acc_scr[...] * corr
    for i in range(ppb):
        p = jnp.exp(scores[i] - m_new)
        l_new = l_new + jnp.sum(p, axis=-1, keepdims=True)
        acc = acc + _dot(p.astype(BF16), v_refs[i][...].astype(BF16))
    l_scr[...] = l_new
    acc_scr[...] = acc
    m_scr[...] = m_new

    @pl.when(j == pl.num_programs(1) - 1)
    def _():
        k_self = jnp.concatenate([ks_ref[...], ks_ref[...]], axis=0).astype(BF16).astype(F32)
        v_self = jnp.concatenate([vs_ref[...], vs_ref[...]], axis=0).astype(BF16).astype(F32)
        s_self = jnp.sum(qm.astype(F32) * k_self, axis=-1, keepdims=True)
        m_fin = jnp.maximum(m_new, s_self)
        c_fin = jnp.exp(m_new - m_fin)
        p_self = jnp.exp(s_self - m_fin)
        o = (acc * c_fin + p_self * v_self) / (l_new * c_fin + p_self)
        lam = _lambda(lq1[...], lk1[...], lq2[...], lk2[...], lam0)
        o_ref[...] = _diff_finish(o[:A_HEADS], o[A_HEADS:], lam, g_ref[...], lam0)


def _attn_sample(qm, k_self, v_self, cache_k, cache_v, page_table, aw, ppb, lam0):
    batch, n_pages = page_table.shape
    n_phys, page = cache_k.shape[0], cache_k.shape[1]
    ck = cache_k.reshape(n_phys, page * A_HEADS, LANES)
    cv = cache_v.reshape(n_phys, page * A_HEADS, LANES)
    nrow = 2 * A_HEADS

    def page_spec(i):
        return pl.BlockSpec((None, page * A_HEADS, LANES),
                            lambda b, j, pt: (pt[b, j * ppb + i], 0, 0))

    per_b = lambda r: pl.BlockSpec((None, r, LANES), lambda b, j, pt: (b, 0, 0))
    vec = lambda n: pl.BlockSpec((1, n), lambda b, j, pt: (0, 0))
    grid_spec = pltpu.PrefetchScalarGridSpec(
        num_scalar_prefetch=1, grid=(batch, n_pages // ppb),
        in_specs=[per_b(nrow), pl.BlockSpec((nrow, 1), lambda b, j, pt: (0, 0)),
                  per_b(A_HEADS), per_b(A_HEADS),
                  vec(A_QK), vec(A_QK), vec(A_QK), vec(A_QK), vec(A_V)]
                 + [page_spec(i) for i in range(ppb)] * 2,
        out_specs=per_b(A_HEADS),
        scratch_shapes=[pltpu.VMEM((nrow, 1), F32), pltpu.VMEM((nrow, 1), F32),
                        pltpu.VMEM((nrow, LANES), F32)])
    return pl.pallas_call(
        functools.partial(_attn_sample_kernel, ppb=ppb, n_pages=n_pages, page=page, lam0=lam0),
        out_shape=jax.ShapeDtypeStruct((batch, A_HEADS, LANES), F32),
        grid_spec=grid_spec,
        compiler_params=_params("parallel", "arbitrary"),
        name="attn_sample",
    )(page_table, qm, aw["slopes16"], k_self, v_self, aw["lq1"], aw["lk1"], aw["lq2"], aw["lk2"],
      aw["subln_g"], *([ck] * ppb), *([cv] * ppb))


def _merge_kernel(or_ref, r_ref, k_ref, v_ref, zr_ref, oa_ref, za_ref, ga_ref, gr_ref,
                  rk_ref, lng_ref, lnb_ref, wr_ref, wa_ref, o_ref):
    gmat = _head_group_matrix()
    o_r = or_ref[...]
    inv_n = 1.0 / R_HEAD
    mean = _head_sums(o_r, gmat) * inv_n
    d = o_r - mean
    var = _head_sums(d * d, gmat) * inv_n
    v = v_ref[...]
    bonus = _head_sums(r_ref[...] * k_ref[...] * rk_ref[...], gmat)
    o_r = d * lax.rsqrt(var + GN_EPS) * lng_ref[...] + lnb_ref[...] + bonus * v
    x_r = (o_r * _silu(zr_ref[...])).astype(BF16)
    x_a = (oa_ref[...] * _silu(za_ref[...])).astype(BF16)
    y_r = _dot(x_r, wr_ref[...])
    y_a = _dot(x_a, wa_ref[...])
    o_ref[...] = (_sigmoid(ga_ref[...]) * y_a + _sigmoid(gr_ref[...]) * y_r).astype(o_ref.dtype)


def _merge(o_r, r, k, v, o_a, zg, mw, tm):
    rows, rw = o_r.shape
    d = mw["w_r_up"].shape[1]
    tile = pl.BlockSpec((tm, rw), lambda i: (i, 0))
    const = lambda shape: pl.BlockSpec(shape, lambda i: (0,) * len(shape))
    return pl.pallas_call(
        _merge_kernel,
        out_shape=jax.ShapeDtypeStruct((rows, d), BF16),
        grid=(rows // tm,),
        in_specs=[tile, tile, tile, tile,
                  pl.BlockSpec((tm, rw), lambda i: (i, 0)),
                  tile,
                  pl.BlockSpec((tm, rw), lambda i: (i, 1)),
                  pl.BlockSpec((tm, d), lambda i: (i, 1)),
                  pl.BlockSpec((tm, d), lambda i: (i, 2)),
                  const((1, rw)), const((1, rw)), const((1, rw)),
                  const(mw["w_r_up"].shape), const(mw["w_a_up"].shape)],
        out_specs=pl.BlockSpec((tm, d), lambda i: (i, 0)),
        compiler_params=_params("parallel"), name="merge",
    )(o_r, r, k, v, zg, o_a, zg, zg, zg, mw["r_k"], mw["lnx_g"], mw["lnx_b"],
      mw["w_r_up"], mw["w_a_up"])


def _out_kernel(x_ref, m_ref, p_ref, wo_ref, wp_ref, wg_ref, g_ref, o_ref):
    h = x_ref[...] + _dot(m_ref[...], wo_ref[...])
    ple = _dot(p_ref[...].astype(BF16), wp_ref[...])
    h = h + ple * _sigmoid(_dot(h.astype(BF16), wg_ref[...]))
    y = h * lax.rsqrt(jnp.mean(h * h, axis=-1, keepdims=True) + RMS_EPS)
    o_ref[...] = y * g_ref[...]


def _out_stage(x, m, p, ow, tm):
    rows, d = x.shape
    pd = p.shape[1]
    const = lambda shape: pl.BlockSpec(shape, lambda i: (0,) * len(shape),
                                       pipeline_mode=pl.Buffered(1))
    return pl.pallas_call(
        _out_kernel,
        out_shape=jax.ShapeDtypeStruct((rows, d), F32),
        grid=(rows // tm,),
        in_specs=[pl.BlockSpec((tm, d), lambda i: (i, 0)), pl.BlockSpec((tm, d), lambda i: (i, 0)),
                  pl.BlockSpec((tm, pd), lambda i: (i, 0)),
                  const((d, d)), const((pd, d)), const((d, d)), const((1, d))],
        out_specs=pl.BlockSpec((tm, d), lambda i: (i, 0)),
        compiler_params=_params("parallel"), name="out_stage",
    )(x, m, p, ow["w_out"], ow["w_ple"], ow["w_ple_gate"], ow["norm_final_g"])


def _tile(n, pref):
    return pref if n % pref == 0 else n


def kernel(x_prompt, x_sample, cache_k, cache_v, state_wkv, state_shift, page_table, p_prompt,
           p_sample, norm_in_g, w_in, mu_shift, w0, w2, a0, a2, k_k, k_a, r_k, lnx_g, lnx_b,
           w_rwkv_up, lambda_q1, lambda_k1, lambda_q2, lambda_k2, subln_g, w_attn_up, w_out,
           w_ple, w_ple_gate, norm_final_g):
    depth = w_in.shape[0]
    assert depth == 1, "single-layer trunk"
    layer = 0
    lam0 = 0.8 - 0.6 * math.exp(-0.3 * layer)
    bp, tp, d = x_prompt.shape
    bs, ts, _ = x_sample.shape
    assert ts == 1
    rw = w0.shape[1]
    sw = mu_shift.shape[1]
    aqk = A_HEADS * 2 * A_QK
    aw_ = A_HEADS * A_V
    row = lambda x: x.reshape(1, -1)

    win = w_in[layer]
    cuts = [0, sw, sw + rw, sw + rw + aqk, sw + rw + 2 * aqk, sw + rw + 2 * aqk + aw_,
            sw + rw + 2 * aqk + 2 * aw_, win.shape[1]]
    seg = lambda i: win[:, cuts[i]:cuts[i + 1]]
    w_sh = seg(0).astype(BF16)
    w_q, w_k, w_v = seg(2).astype(BF16), seg(3).astype(BF16), seg(4).astype(BF16)
    w_zg = jnp.concatenate([seg(1), seg(5), seg(6)], axis=1).astype(BF16)

    pw = dict(mu=row(mu_shift[layer]), w0=row(w0[layer]), w2=w2[layer].astype(BF16),
              a0=row(a0[layer]), a2=a2[layer].astype(BF16), k_k=row(k_k[layer]), k_a=row(k_a[layer]))
    slopes = 2.0 ** (-8.0 * jnp.arange(1, A_HEADS + 1, dtype=F32) / A_HEADS)
    aw = dict(slopes=slopes, slopes16=jnp.tile(slopes, 2).reshape(2 * A_HEADS, 1),
              lq1=row(lambda_q1[layer]), lk1=row(lambda_k1[layer]), lq2=row(lambda_q2[layer]),
              lk2=row(lambda_k2[layer]), subln_g=row(subln_g[layer]))
    mw = dict(r_k=row(r_k[layer]), lnx_g=row(lnx_g[layer]), lnx_b=row(lnx_b[layer]),
              w_r_up=w_rwkv_up[layer].astype(BF16), w_a_up=w_attn_up[layer].astype(BF16))
    ow = dict(w_out=w_out[layer].astype(BF16), w_ple=w_ple[layer].astype(BF16),
              w_ple_gate=w_ple_gate[layer].astype(BF16), norm_final_g=row(norm_final_g))

    def project(x2d, tm):
        xn = _rms_cast(x2d, norm_in_g[layer], _tile(x2d.shape[0], 512))
        sh = _matmul(xn, w_sh, F32, tm, 640, name="proj_shift")
        q = _matmul(xn, w_q, BF16, tm, 1024, scale=A_QK ** -0.5, name="proj_q")
        k = _matmul(xn, w_k, F32, tm, 1024, name="proj_k")
        v = _matmul(xn, w_v, F32, tm, 1024, name="proj_v")
        zg = _matmul(xn, w_zg, F32, tm, 1024, name="proj_gates")
        return sh, q, k, v, zg

    xp = x_prompt.reshape(bp * tp, d)
    sh, q, k, v, zg = project(xp, _tile(bp * tp, 1024))
    r_, lw_, k_, v_, al_, be_ = _rwkv_prep(sh, None, tp, _tile(tp, 256), pw)
    o_r, wkv_p = _rwkv_scan(r_, lw_, k_, v_, al_, be_, bp, tp, _tile(tp, 1024), _tile(tp, 512))
    o_a = _attn_prompt(q, k, v, aw, bp, tp, _tile(tp, 512), lam0)
    m = _merge(o_r, r_, k_, v_, o_a, zg, mw, _tile(bp * tp, 256))
    y_p = _out_stage(xp, m, p_prompt[layer].reshape(bp * tp, -1), ow, _tile(bp * tp, 256))
    y_prompt = y_p.reshape(bp, tp, d)
    k_prompt = k.reshape(1, bp, tp, A_HEADS, 2 * A_QK)
    v_prompt = v.reshape(1, bp, tp, A_HEADS, A_V)
    shift_prompt = sh.reshape(bp, tp, sw)[:, -1][None]

    xs = x_sample.reshape(bs, d)
    sh_s, q_s, k_s, v_s, zg_s = project(xs, bs)
    r_, lw_, k_, v_, al_, be_ = _rwkv_prep(sh_s, state_shift[layer], 1, bs, pw)
    o_r, wkv_s = _rwkv_step(state_wkv[layer], r_, lw_, k_, v_, al_, be_)
    q_h = q_s.reshape(bs, 1, A_HEADS, 2, A_QK)
    sel = jnp.eye(2, dtype=q_s.dtype).reshape(1, 2, 1, 2, 1)
    qm = (q_h * sel).reshape(bs, 2 * A_HEADS, 2 * A_QK)
    o_a = _attn_sample(qm, k_s.reshape(bs, A_HEADS, 2 * A_QK), v_s.reshape(bs, A_HEADS, A_V),
                       cache_k[layer], cache_v[layer], page_table, aw,
                       8 if page_table.shape[1] % 8 == 0 else 1, lam0)
    m = _merge(o_r, r_, k_, v_, o_a.reshape(bs, aw_), zg_s, mw, bs)
    y_s = _out_stage(xs, m, p_sample[layer].reshape(bs, -1), ow, bs)
    y_sample = y_s.reshape(bs, ts, d)
    k_sample = k_s.reshape(1, bs, ts, A_HEADS, 2 * A_QK)
    v_sample = v_s.reshape(1, bs, ts, A_HEADS, A_V)

    return (y_prompt, y_sample, k_prompt, v_prompt, k_sample, v_sample,
            wkv_p[None], wkv_s[None], shift_prompt, sh_s[None])
```

```python
import functools
import math

import jax
import jax.numpy as jnp
from jax import lax
from jax.experimental import pallas as pl
from jax.experimental.pallas import tpu as pltpu

F32 = jnp.float32
BF16 = jnp.bfloat16

LANES = 128
A_HEADS = 8
A_QK = 64
A_V = 2 * A_QK
R_HEAD = 64
RMS_EPS = 1e-5
GN_EPS = 64e-5
NEG_INF = -1e30
RWKV_CHUNK = 64
VMEM_LIMIT = 56 * 1024 * 1024

NT_DIMS = (((1,), (1,)), ((), ()))
TN_DIMS = (((0,), (0,)), ((), ()))


def _params(*sem):
    return pltpu.CompilerParams(dimension_semantics=sem, vmem_limit_bytes=VMEM_LIMIT)


def _dot(a, b):
    return jnp.dot(a, b, preferred_element_type=F32)


def _dg(a, b, dims):
    return lax.dot_general(a, b, dims, preferred_element_type=F32)


def _split_dot_rhs(a_bf16, x):
    hi = x.astype(BF16)
    lo = (x - hi.astype(F32)).astype(BF16)
    return _dot(a_bf16, hi) + _dot(a_bf16, lo)


def _split_dot_lhs(x, b_bf16):
    hi = x.astype(BF16)
    lo = (x - hi.astype(F32)).astype(BF16)
    return _dot(hi, b_bf16) + _dot(lo, b_bf16)


def _sigmoid(x):
    return 1.0 / (1.0 + jnp.exp(-x))


def _silu(x):
    return x * _sigmoid(x)


def _head_group_matrix():
    i = lax.broadcasted_iota(jnp.int32, (LANES, LANES), 0)
    j = lax.broadcasted_iota(jnp.int32, (LANES, LANES), 1)
    return ((i // R_HEAD) == (j // R_HEAD)).astype(BF16)


def _head_sums(x, gmat):
    cols = x.shape[1] // LANES
    parts = [_split_dot_lhs(x[:, c * LANES:(c + 1) * LANES], gmat) for c in range(cols)]
    return jnp.concatenate(parts, axis=1)


def _rms_kernel(x_ref, g_ref, o_ref):
    x = x_ref[...]
    y = x * lax.rsqrt(jnp.mean(x * x, axis=-1, keepdims=True) + RMS_EPS)
    o_ref[...] = (y * g_ref[...]).astype(o_ref.dtype)


def _rms_cast(x, g, tm):
    m, d = x.shape
    return pl.pallas_call(
        _rms_kernel,
        out_shape=jax.ShapeDtypeStruct((m, d), BF16),
        grid=(m // tm,),
        in_specs=[pl.BlockSpec((tm, d), lambda i: (i, 0)),
                  pl.BlockSpec((1, d), lambda i: (0, 0))],
        out_specs=pl.BlockSpec((tm, d), lambda i: (i, 0)),
        compiler_params=_params("parallel"),
        name="rms_cast",
    )(x, g.reshape(1, d))


def _mm_kernel(a_ref, b_ref, o_ref, *, scale):
    acc = _dot(a_ref[...], b_ref[...])
    if scale != 1.0:
        acc = acc * scale
    o_ref[...] = acc.astype(o_ref.dtype)


def _matmul(a, b, col0, n, out_dtype, tm, tn, scale=1.0, name="matmul"):
    m, k = a.shape
    return pl.pallas_call(
        functools.partial(_mm_kernel, scale=scale),
        out_shape=jax.ShapeDtypeStruct((m, n), out_dtype),
        grid=(m // tm, n // tn),
        in_specs=[pl.BlockSpec((tm, k), lambda i, j: (i, 0)),
                  pl.BlockSpec((pl.Element(k), pl.Element(tn)),
                               lambda i, j: (0, pl.multiple_of(col0 + j * tn, LANES)))],
        out_specs=pl.BlockSpec((tm, tn), lambda i, j: (i, j)),
        compiler_params=_params("parallel", "parallel"),
        name=name,
    )(a, b)


def _rwkv_token_math(sh3, lora, prev3, prev_lora, mu3, mu_lora, w0, w2p, a0, a2p, k_k, k_a, r_k):
    lerp = lambda x, p, mu: x + mu * (p - x)
    r, kr, vr = (lerp(x, p, mu) for x, p, mu in zip(sh3, prev3, mu3))
    lo = lerp(lora, prev_lora, mu_lora)
    z = w0 + _dot(jnp.tanh(lo).astype(BF16), w2p)
    softplus = jnp.maximum(-z, 0.0) + jnp.log1p(jnp.exp(-jnp.abs(z)))
    logw = -jnp.exp(-softplus - 0.5)
    a = _sigmoid(a0 + _dot(lo.astype(BF16), a2p))
    kk = kr * k_k
    gmat = _head_group_matrix()
    norm = jnp.maximum(jnp.sqrt(_head_sums(kk * kk, gmat)), 1e-12)
    kk = kk / norm
    kmod = kr * (1.0 + (a - 1.0) * k_a)
    bonus = _head_sums(r * kmod * r_k, gmat) * vr
    return r, logw, kmod, vr, -kk, kk * a, bonus


def _rwkv_prep_sample_kernel(sh_ref, prev_ref, mu_ref, w0_ref, w2_ref, a0_ref, a2_ref,
                             kk_ref, ka_ref, rk_ref, r_o, lw_o, k_o, v_o, al_o, be_o, bo_o):
    rw = w0_ref.shape[1]
    win3 = lambda ref: tuple(ref[:, i * rw:(i + 1) * rw] for i in range(3))
    lo = lambda ref: ref[:, 3 * rw:]
    outs = _rwkv_token_math(win3(sh_ref), lo(sh_ref), win3(prev_ref), lo(prev_ref), win3(mu_ref),
                            lo(mu_ref), w0_ref[...], w2_ref[...], a0_ref[...], a2_ref[...],
                            kk_ref[...], ka_ref[...], rk_ref[...])
    for ref, val in zip((r_o, lw_o, k_o, v_o, al_o, be_o, bo_o), outs):
        ref[...] = val


def _rwkv_prep_sample(sh, prev, pw):
    rows, sw = sh.shape
    rw = pw["w0"].shape[1]
    const = lambda shape: pl.BlockSpec(shape, lambda *_: (0,) * len(shape))
    vec = const((1, rw))
    return pl.pallas_call(
        _rwkv_prep_sample_kernel, out_shape=[jax.ShapeDtypeStruct((rows, rw), F32)] * 7, grid=(1,),
        in_specs=[const((rows, sw)), const((rows, sw)), const((1, sw)), vec, const(pw["w2p"].shape), vec,
                  const(pw["a2p"].shape), vec, vec, vec],
        out_specs=[const((rows, rw))] * 7,
        compiler_params=_params("arbitrary"), name="rwkv_prep_sample",
    )(sh, prev, pw["mu"], pw["w0"], pw["w2p"], pw["a0"], pw["a2p"], pw["k_k"], pw["k_a"], pw["r_k"])


def _rwkv_chunk_kernel(r_sh, k_sh, v_sh, lo_sh, r_tl, k_tl, v_tl, lo_tl, mu_r, mu_k, mu_v, mu_lo,
                       w0_ref, w2_ref, a0_ref, a2_ref, kk_ref, ka_ref, rk_ref,
                       m_o, n_o, lr_o, op_o, wc_o, bo_o, *, n_chunks, tiles_per_seq):
    seq_start = pl.program_id(0) % tiles_per_seq == 0
    trow = lax.broadcasted_iota(jnp.int32, r_sh.shape, 0)

    def prev_of(ref, tail_ref):
        first = jnp.where(seq_start, 0.0, tail_ref[7:8, :])
        return jnp.where(trow == 0, first, pltpu.roll(ref[...], 1, axis=0))

    r_all, lw_all, k_all, v_all, al_all, be_all, bonus = _rwkv_token_math(
        (r_sh[...], k_sh[...], v_sh[...]), lo_sh[...],
        (prev_of(r_sh, r_tl), prev_of(k_sh, k_tl), prev_of(v_sh, v_tl)), prev_of(lo_sh, lo_tl),
        (mu_r[...], mu_k[...], mu_v[...]), mu_lo[...], w0_ref[...], w2_ref[...], a0_ref[...],
        a2_ref[...], kk_ref[...], ka_ref[...], rk_ref[...])
    bo_o[...] = bonus

    c = RWKV_CHUNK
    two_c = 2 * c
    row = lax.broadcasted_iota(jnp.int32, (two_c, LANES), 0)
    col = lax.broadcasted_iota(jnp.int32, (two_c, LANES), 1)
    keep = (row >= c) == (col >= R_HEAD)
    strict = col < row
    incl = col <= row
    eye = (row == col).astype(F32)
    ti = lax.broadcasted_iota(jnp.int32, (c, c), 0)
    si = lax.broadcasted_iota(jnp.int32, (c, c), 1)
    ltri = (si <= ti).astype(BF16)

    def stack(x):
        return jnp.where(keep, jnp.concatenate([x, x], axis=0), 0.0).astype(BF16)

    cs = range(n_chunks)
    each = lambda fn, *lists: [fn(*xs) for xs in zip(*lists)]
    split = lambda x: [x[ci * c:(ci + 1) * c, :] for ci in cs]
    r, lw, k, v, al, be = (split(x) for x in (r_all, lw_all, k_all, v_all, al_all, be_all))
    cl = each(lambda x: _split_dot_rhs(ltri, x), lw)
    e_pos = each(jnp.exp, cl)
    e_neg = each(lambda x: jnp.exp(-x), cl)
    e_prev = each(lambda x, y: jnp.exp(x - y), cl, lw)
    wc = each(lambda x: x[c - 1:c, :], e_pos)
    kt = each(jnp.multiply, k, e_neg)
    bt = each(jnp.multiply, be, e_neg)
    la = each(lambda x, y: stack(x * y), al, e_prev)
    lr = each(lambda x, y: stack(x * y), r, e_pos)
    rb, rk, vs = each(stack, bt), each(stack, kt), each(stack, v)
    rbw = each(lambda x, y: stack(x * y), bt, wc)
    rkw = each(lambda x, y: stack(x * y), kt, wc)
    nt_dot = lambda x, y: _dg(x, y, NT_DIMS)
    a_ab = each(lambda x, y: jnp.where(strict, nt_dot(x, y), 0.0), la, rb)
    a_ak = each(lambda x, y: jnp.where(strict, nt_dot(x, y), 0.0).astype(BF16), la, rk)
    a_rb = each(lambda x, y: jnp.where(incl, nt_dot(x, y), 0.0).astype(BF16), lr, rb)
    a_rk = each(lambda x, y: jnp.where(incl, nt_dot(x, y), 0.0).astype(BF16), lr, rk)
    p = each(lambda x: eye + x, a_ab)
    pw = a_ab
    for _ in range(int(math.log2(c)) - 1):
        pwb = each(lambda x: x.astype(BF16), pw)
        pw = each(_dot, pwb, pwb)
        p = each(lambda x, y: x + _dot(x.astype(BF16), y.astype(BF16)), p, pw)
    tb = each(lambda x: x.astype(BF16), p)
    ua = each(lambda x, y: _dot(x, y).astype(BF16), tb, la)
    akv = each(lambda x, y: _dot(x, y).astype(BF16), a_ak, vs)
    uv = each(lambda x, y: _dot(x, y).astype(BF16), tb, akv)
    lr2 = each(lambda x, y, z: (x.astype(F32) + _dot(y, z)).astype(BF16), lr, a_rb, ua)
    op = each(lambda x, y, z, w: _dot(x, y) + _dot(z, w), a_rb, uv, a_rk, vs)
    m2 = each(lambda x, y: _dg(x, y, TN_DIMS).astype(BF16), ua, rbw)
    n2 = each(lambda x, y, z, w: _dg(jnp.concatenate([x, y], axis=0),
                                     jnp.concatenate([z, w], axis=0), TN_DIMS), uv, vs, rbw, rkw)
    for ci in cs:
        lr_o[ci] = lr2[ci]
        op_o[pl.ds(ci * c, c), :] = op[ci][:c] + op[ci][c:]
        m_o[ci] = m2[ci]
        n_o[ci] = n2[ci]
        wc_o[ci] = wc[ci]


def _rwkv_state_kernel(m_ref, n_ref, lr_ref, op_ref, wc_ref, o_ref, s_ref, s_scr, *, n_chunks, pairs):
    c = RWKV_CHUNK

    @pl.when(pl.program_id(1) == 0)
    def _():
        s_scr[...] = jnp.zeros_like(s_scr)

    def body(ci, carry):
        sl = pl.ds(pl.multiple_of(ci * c, c), c)
        outs = []
        for p in range(pairs):
            s = s_scr[p]
            sb = s.astype(BF16)
            o = _dg(lr_ref[ci, p], sb, NT_DIMS)
            outs.append(o[:c] + o[c:])
            s_scr[p] = s * wc_ref[ci, p] + _dot(sb, m_ref[ci, p]) + n_ref[ci, p]
        o_ref[sl, :] = jnp.concatenate(outs, axis=1) + op_ref[sl, :]
        return carry

    lax.fori_loop(0, n_chunks, body, 0)
    s_ref[...] = s_scr[...]


def _rwkv_scan(sh, pw, batch, seq_len, tt_chunk, tt_state):
    rows = sh.shape[0]
    rw = pw["w0"].shape[1]
    pairs = rw // LANES
    c = RWKV_CHUNK
    n_all = rows // c
    nb = tt_chunk // c
    gcols = rw // LANES
    tile = pl.BlockSpec((tt_chunk, LANES), lambda i, p: (i, p))
    mat = pl.BlockSpec((nb, None, LANES, LANES), lambda i, p: (i, p, 0, 0))
    win = lambda g: pl.BlockSpec((tt_chunk, LANES), lambda i, p: (i, g * gcols + p))
    lora = pl.BlockSpec((tt_chunk, LANES), lambda i, p: (i, 3 * gcols))
    trow = lambda i: jnp.maximum(i * (tt_chunk // 8) - 1, 0)
    twin = lambda g: pl.BlockSpec((8, LANES), lambda i, p: (trow(i), g * gcols + p))
    tlora = pl.BlockSpec((8, LANES), lambda i, p: (trow(i), 3 * gcols))
    mwin = lambda g: pl.BlockSpec((1, LANES), lambda i, p: (0, g * gcols + p))
    mlora = pl.BlockSpec((1, LANES), lambda i, p: (0, 3 * gcols))
    vec = pl.BlockSpec((1, LANES), lambda i, p: (0, p))
    up = pl.BlockSpec((LANES, LANES), lambda i, p: (0, p))
    m_, n_, lr_, op_, wc_, bonus = pl.pallas_call(
        functools.partial(_rwkv_chunk_kernel, n_chunks=nb, tiles_per_seq=seq_len // tt_chunk),
        out_shape=[jax.ShapeDtypeStruct((n_all, pairs, LANES, LANES), BF16),
                   jax.ShapeDtypeStruct((n_all, pairs, LANES, LANES), F32),
                   jax.ShapeDtypeStruct((n_all, pairs, 2 * c, LANES), BF16),
                   jax.ShapeDtypeStruct((rows, rw), F32),
                   jax.ShapeDtypeStruct((n_all, pairs, 1, LANES), F32),
                   jax.ShapeDtypeStruct((rows, rw), F32)],
        grid=(rows // tt_chunk, pairs),
        in_specs=[win(0), win(1), win(2), lora, twin(0), twin(1), twin(2), tlora,
                  mwin(0), mwin(1), mwin(2), mlora, vec, up, vec, up, vec, vec, vec],
        out_specs=[mat, mat, pl.BlockSpec((nb, None, 2 * c, LANES), lambda i, p: (i, p, 0, 0)), tile,
                   pl.BlockSpec((nb, None, 1, LANES), lambda i, p: (i, p, 0, 0)), tile],
        compiler_params=_params("parallel", "parallel"),
        name="rwkv_chunk",
    )(sh, sh, sh, sh, sh, sh, sh, sh, pw["mu"], pw["mu"], pw["mu"], pw["mu"],
      pw["w0"], pw["w2p"], pw["a0"], pw["a2p"], pw["k_k"], pw["k_a"], pw["r_k"])

    nt = seq_len // tt_state
    ns = tt_state // c
    smat = lambda rws: pl.BlockSpec((ns, pairs, rws, LANES), lambda b, i: (b * nt + i, 0, 0, 0))
    wide = pl.BlockSpec((tt_state, rw), lambda b, i: (b * nt + i, 0))
    o, s = pl.pallas_call(
        functools.partial(_rwkv_state_kernel, n_chunks=ns, pairs=pairs),
        out_shape=[jax.ShapeDtypeStruct((rows, rw), F32),
                   jax.ShapeDtypeStruct((batch, pairs, LANES, LANES), F32)],
        grid=(batch, nt),
        in_specs=[smat(LANES), smat(LANES), smat(2 * c), wide, smat(1)],
        out_specs=[wide, pl.BlockSpec((None, pairs, LANES, LANES), lambda b, i: (b, 0, 0, 0))],
        scratch_shapes=[pltpu.VMEM((pairs, LANES, LANES), F32)],
        compiler_params=_params("parallel", "arbitrary"),
        name="rwkv_state",
    )(m_, n_, lr_, op_, wc_)
    s = s.reshape(batch, pairs, 2, R_HEAD, 2, R_HEAD)
    s = jnp.stack([s[:, :, 0, :, 0, :], s[:, :, 1, :, 1, :]], axis=2)
    return o, bonus, s.reshape(batch, 2 * pairs, R_HEAD, R_HEAD)


def _rwkv_step_kernel(s_ref, r_ref, lw_ref, k_ref, al_ref, be_ref, vcol_ref, o_ref, so_ref):
    s = s_ref[...]
    sa = jnp.sum(s * al_ref[...], axis=-1, keepdims=True)
    s_new = s * jnp.exp(lw_ref[...]) + sa * be_ref[...] + vcol_ref[...] * k_ref[...]
    so_ref[...] = s_new
    o_ref[...] = jnp.sum(s_new * r_ref[...], axis=-1, keepdims=True)


def _rwkv_step(state, r, lw, k, v, al, be):
    b, h, n, _ = state.shape
    vec = lambda x: x.reshape(b, h, 1, n)
    vspec = pl.BlockSpec((None, h, 1, n), lambda i: (i, 0, 0, 0))
    cspec = pl.BlockSpec((None, h, n, 1), lambda i: (i, 0, 0, 0))
    sspec = pl.BlockSpec((None, h, n, n), lambda i: (i, 0, 0, 0))
    o, s_new = pl.pallas_call(
        _rwkv_step_kernel,
        out_shape=[jax.ShapeDtypeStruct((b, h, n, 1), F32), jax.ShapeDtypeStruct(state.shape, F32)],
        grid=(b,),
        in_specs=[sspec, vspec, vspec, vspec, vspec, vspec, cspec],
        out_specs=[cspec, sspec],
        compiler_params=_params("parallel"), name="rwkv_step",
    )(state, vec(r), vec(lw), vec(k), vec(al), vec(be), v.reshape(b, h, n, 1))
    return o.reshape(b, h * n), s_new


def _lambda(lq1, lk1, lq2, lk2, lam0):
    t1 = jnp.sum(lq1 * lk1, axis=-1, keepdims=True)
    t2 = jnp.sum(lq2 * lk2, axis=-1, keepdims=True)
    return jnp.exp(t1) - jnp.exp(t2) + lam0


def _diff_finish(o1, o2, lam, subln_g, lam0):
    o = o1 - lam * o2
    y = o * lax.rsqrt(jnp.mean(o * o, axis=-1, keepdims=True) + RMS_EPS)
    return y * subln_g * (1.0 - lam0)


ATTN_GROUP = 256
ATTN_KEYS = 512
ATTN_AHEAD = 2
ALIBI_SPLIT = 256


def _attn_prompt_kernel(slopes_ref, q_ref, k_ref, v_ref, lq1, lk1, lq2, lk2, g_ref, o_ref,
                        kb, vt, *, tq, lam0):
    h = pl.program_id(1)
    qi = pl.program_id(2)
    seq_len = k_ref.shape[0]

    @pl.when(qi == 0)
    def _():
        slope = slopes_ref[h]
        pos = lax.broadcasted_iota(jnp.int32, (seq_len, LANES), 0)
        lane = lax.broadcasted_iota(jnp.int32, (seq_len, LANES), 1)
        lo = (pos % ALIBI_SPLIT).astype(F32) * slope
        hi = (pos - pos % ALIBI_SPLIT).astype(F32) * slope
        kb[:, :LANES] = k_ref[...].astype(BF16)
        kb[:, LANES:] = jnp.where(lane == 0, lo, jnp.where(lane == 1, hi, 0.0)).astype(BF16)
        for c0 in range(0, seq_len, tq):
            vt[:, c0:c0 + tq] = v_ref[c0:c0 + tq, :].T.astype(BF16)

    lane = lax.broadcasted_iota(jnp.int32, (tq, LANES), 1)
    q = q_ref[...]
    ones = jnp.where(lane < 2, 1.0, 0.0).astype(BF16)
    zero = jnp.zeros_like(q)
    q_aug = jnp.concatenate([
        jnp.concatenate([jnp.where(lane < A_QK, q, zero), ones], axis=1),
        jnp.concatenate([jnp.where(lane >= A_QK, q, zero), ones], axis=1)], axis=0)
    gw = ATTN_GROUP
    groups = range(2 * tq // gw)
    q_grp = [q_aug[g * gw:(g + 1) * gw, :] for g in groups]

    ksub = min(ATTN_KEYS, tq)

    def block(j, carry, diagonal):
        m, l, acc = (list(x) for x in carry)
        steps = []
        for a in range(tq // ksub):
            for g in groups:
                q0 = (g * gw) % tq
                rows = min(ksub, q0 + gw - a * ksub) if diagonal else ksub
                if rows > 0:
                    steps.append((a, g, rows, diagonal and (a * ksub + rows - 1 > q0)))

        def scores(a, g, rows, masked):
            ks = pl.ds(pl.multiple_of(j * tq + a * ksub, ksub), rows)
            s = _dg(kb[ks, :], q_grp[g], NT_DIMS)
            if masked:
                r_i = lax.broadcasted_iota(jnp.int32, (rows, gw), 0) + a * ksub
                c_i = lax.broadcasted_iota(jnp.int32, (rows, gw), 1) + (g * gw) % tq
                s = jnp.where(r_i <= c_i, s, NEG_INF)
            return s

        ahead = [scores(*st) for st in steps[:ATTN_AHEAD]]
        for idx, (a, g, rows, _) in enumerate(steps):
            if idx + ATTN_AHEAD < len(steps):
                ahead.append(scores(*steps[idx + ATTN_AHEAD]))
            s = ahead.pop(0)
            ks = pl.ds(pl.multiple_of(j * tq + a * ksub, ksub), rows)
            m_new = jnp.maximum(m[g], jnp.max(s, axis=0, keepdims=True))
            p = jnp.exp(s - m_new)
            corr = jnp.exp(m[g] - m_new)
            l[g] = l[g] * corr + jnp.sum(p, axis=0, keepdims=True)
            acc[g] = acc[g] * corr + _dot(vt[:, ks], p.astype(BF16))
            m[g] = m_new
        return tuple(m), tuple(l), tuple(acc)

    init = (tuple(jnp.full((1, gw), NEG_INF, F32) for _ in groups),
            tuple(jnp.zeros((1, gw), F32) for _ in groups),
            tuple(jnp.zeros((LANES, gw), F32) for _ in groups))
    carry = lax.fori_loop(0, qi, lambda j, cr: block(j, cr, False), init)
    _, l, acc = block(qi, carry, True)

    o = jnp.concatenate([a / x for a, x in zip(acc, l)], axis=1).T
    lam = _lambda(lq1[...], lk1[...], lq2[...], lk2[...], lam0)
    o_ref[...] = _diff_finish(o[:tq], o[tq:], lam, g_ref[...], lam0)


def _attn_prompt(q, k, v, aw, batch, seq_len, tq, lam0):
    rows, width = k.shape
    nq = seq_len // tq
    qspec = pl.BlockSpec((tq, LANES), lambda b, h, i, *_: (b * nq + i, h))
    kvspec = pl.BlockSpec((seq_len, LANES), lambda b, h, i, *_: (b, h))
    vec = lambda n: pl.BlockSpec((1, n), lambda b, h, i, *_: (0, 0))
    grid_spec = pltpu.PrefetchScalarGridSpec(
        num_scalar_prefetch=1, grid=(batch, A_HEADS, nq),
        in_specs=[qspec, kvspec, kvspec, vec(A_QK), vec(A_QK), vec(A_QK), vec(A_QK), vec(A_V)],
        out_specs=qspec,
        scratch_shapes=[pltpu.VMEM((seq_len, 2 * LANES), BF16), pltpu.VMEM((LANES, seq_len), BF16)])
    return pl.pallas_call(
        functools.partial(_attn_prompt_kernel, tq=tq, lam0=lam0),
        out_shape=jax.ShapeDtypeStruct((rows, width), F32),
        grid_spec=grid_spec,
        compiler_params=_params("parallel", "parallel", "arbitrary"),
        name="attn_prompt",
    )(aw["slopes"], q, k, v, aw["lq1"], aw["lk1"], aw["lq2"], aw["lk2"], aw["subln_g"])


def _attn_sample_kernel(pt_ref, qm_ref, slope_ref, ks_ref, vs_ref, lq1, lk1, lq2, lk2, g_ref, *rest,
                        ppb, n_pages, page, lam0):
    k_refs = rest[:ppb]
    v_refs = rest[ppb:2 * ppb]
    o_ref = rest[2 * ppb]
    m_scr, l_scr, acc_scr = rest[2 * ppb + 1:]
    j = pl.program_id(1)
    nrow = 2 * A_HEADS
    pcols = page * A_HEADS

    @pl.when(j == 0)
    def _():
        m_scr[...] = jnp.full(m_scr.shape, NEG_INF, F32)
        l_scr[...] = jnp.zeros_like(l_scr)
        acc_scr[...] = jnp.zeros_like(acc_scr)

    qm = qm_ref[...]
    slope = slope_ref[...]
    r_i = lax.broadcasted_iota(jnp.int32, (nrow, pcols), 0)
    c_i = lax.broadcasted_iota(jnp.int32, (nrow, pcols), 1)
    valid = (c_i % A_HEADS) == (r_i % A_HEADS)
    tok = (c_i // A_HEADS).astype(F32)
    scores = []
    for i in range(ppb):
        base = ((n_pages - (j * ppb + i)) * page).astype(F32)
        s = _dg(qm, k_refs[i][...].astype(BF16), NT_DIMS) - slope * (base - tok)
        scores.append(jnp.where(valid, s, NEG_INF))
    m_old = m_scr[...]
    m_blk = functools.reduce(jnp.maximum, [jnp.max(s, axis=-1, keepdims=True) for s in scores])
    m_new = jnp.maximum(m_old, m_blk)
    corr = jnp.exp(m_old - m_new)
    l_new = l_scr[...] * corr
    acc = acc_scr[...] * corr
    for i in range(ppb):
        p = jnp.exp(scores[i] - m_new)
        l_new = l_new + jnp.sum(p, axis=-1, keepdims=True)
        acc = acc + _dot(p.astype(BF16), v_refs[i][...].astype(BF16))
    l_scr[...] = l_new
    acc_scr[...] = acc
    m_scr[...] = m_new

    @pl.when(j == pl.num_programs(1) - 1)
    def _():
        k_self = jnp.concatenate([ks_ref[...], ks_ref[...]], axis=0).astype(BF16).astype(F32)
        v_self = jnp.concatenate([vs_ref[...], vs_ref[...]], axis=0).astype(BF16).astype(F32)
        s_self = jnp.sum(qm.astype(F32) * k_self, axis=-1, keepdims=True)
        m_fin = jnp.maximum(m_new, s_self)
        c_fin = jnp.exp(m_new - m_fin)
        p_self = jnp.exp(s_self - m_fin)
        o = (acc * c_fin + p_self * v_self) / (l_new * c_fin + p_self)
        lam = _lambda(lq1[...], lk1[...], lq2[...], lk2[...], lam0)
        o_ref[...] = _diff_finish(o[:A_HEADS], o[A_HEADS:], lam, g_ref[...], lam0)


def _attn_sample(qm, k_self, v_self, cache_k, cache_v, page_table, aw, ppb, lam0):
    batch, n_pages = page_table.shape
    n_phys, page = cache_k.shape[0], cache_k.shape[1]
    ck = cache_k.reshape(n_phys, page * A_HEADS, LANES)
    cv = cache_v.reshape(n_phys, page * A_HEADS, LANES)
    nrow = 2 * A_HEADS

    def page_spec(i):
        return pl.BlockSpec((None, page * A_HEADS, LANES),
                            lambda b, j, pt: (pt[b, j * ppb + i], 0, 0))

    per_b = lambda r: pl.BlockSpec((None, r, LANES), lambda b, j, pt: (b, 0, 0))
    vec = lambda n: pl.BlockSpec((1, n), lambda b, j, pt: (0, 0))
    grid_spec = pltpu.PrefetchScalarGridSpec(
        num_scalar_prefetch=1, grid=(batch, n_pages // ppb),
        in_specs=[per_b(nrow), pl.BlockSpec((nrow, 1), lambda b, j, pt: (0, 0)),
                  per_b(A_HEADS), per_b(A_HEADS),
                  vec(A_QK), vec(A_QK), vec(A_QK), vec(A_QK), vec(A_V)]
                 + [page_spec(i) for i in range(ppb)] * 2,
        out_specs=per_b(A_HEADS),
        scratch_shapes=[pltpu.VMEM((nrow, 1), F32), pltpu.VMEM((nrow, 1), F32),
                        pltpu.VMEM((nrow, LANES), F32)])
    return pl.pallas_call(
        functools.partial(_attn_sample_kernel, ppb=ppb, n_pages=n_pages, page=page, lam0=lam0),
        out_shape=jax.ShapeDtypeStruct((batch, A_HEADS, LANES), F32),
        grid_spec=grid_spec,
        compiler_params=_params("parallel", "arbitrary"),
        name="attn_sample",
    )(page_table, qm, aw["slopes16"], k_self, v_self, aw["lq1"], aw["lk1"], aw["lq2"], aw["lk2"],
      aw["subln_g"], *([ck] * ppb), *([cv] * ppb))


def _merge_kernel(or_ref, bo_ref, zr_ref, oa_ref, za_ref, ga_ref, gr_ref,
                  lng_ref, lnb_ref, wr_ref, wa_ref, o_ref):
    gmat = _head_group_matrix()
    o_r = or_ref[...]
    inv_n = 1.0 / R_HEAD
    mean = _head_sums(o_r, gmat) * inv_n
    d = o_r - mean
    var = _head_sums(d * d, gmat) * inv_n
    o_r = d * lax.rsqrt(var + GN_EPS) * lng_ref[...] + lnb_ref[...] + bo_ref[...]
    x_r = (o_r * _silu(zr_ref[...])).astype(BF16)
    x_a = (oa_ref[...] * _silu(za_ref[...])).astype(BF16)
    y_r = _dot(x_r, wr_ref[...])
    y_a = _dot(x_a, wa_ref[...])
    o_ref[...] = (_sigmoid(ga_ref[...]) * y_a + _sigmoid(gr_ref[...]) * y_r).astype(o_ref.dtype)


def _merge(o_r, bonus, o_a, zr, zag, mw, tm):
    rows, rw = o_r.shape
    d = mw["w_r_up"].shape[1]
    tile = pl.BlockSpec((tm, rw), lambda i: (i, 0))
    gate = lambda col0: pl.BlockSpec((pl.Element(tm), pl.Element(d)),
                                     lambda i: (pl.multiple_of(i * tm, 8), col0))
    const = lambda shape: pl.BlockSpec(shape, lambda i: (0,) * len(shape))
    return pl.pallas_call(
        _merge_kernel,
        out_shape=jax.ShapeDtypeStruct((rows, d), BF16),
        grid=(rows // tm,),
        in_specs=[tile, tile, tile, tile, tile, gate(rw), gate(rw + d),
                  const((1, rw)), const((1, rw)),
                  const(mw["w_r_up"].shape), const(mw["w_a_up"].shape)],
        out_specs=pl.BlockSpec((tm, d), lambda i: (i, 0)),
        compiler_params=_params("parallel"), name="merge",
    )(o_r, bonus, zr, o_a, zag, zag, zag, mw["lnx_g"], mw["lnx_b"], mw["w_r_up"], mw["w_a_up"])


def _out_kernel(x_ref, m_ref, p_ref, wo_ref, wp_ref, wg_ref, g_ref, o_ref):
    h = x_ref[...] + _dot(m_ref[...], wo_ref[...])
    ple = _dot(p_ref[...].astype(BF16), wp_ref[...])
    h = h + ple * _sigmoid(_dot(h.astype(BF16), wg_ref[...]))
    y = h * lax.rsqrt(jnp.mean(h * h, axis=-1, keepdims=True) + RMS_EPS)
    o_ref[...] = y * g_ref[...]


def _out_stage(x, m, p, ow, tm):
    rows, d = x.shape
    pd = p.shape[1]
    const = lambda shape: pl.BlockSpec(shape, lambda i: (0,) * len(shape),
                                       pipeline_mode=pl.Buffered(1))
    return pl.pallas_call(
        _out_kernel,
        out_shape=jax.ShapeDtypeStruct((rows, d), F32),
        grid=(rows // tm,),
        in_specs=[pl.BlockSpec((tm, d), lambda i: (i, 0)), pl.BlockSpec((tm, d), lambda i: (i, 0)),
                  pl.BlockSpec((tm, pd), lambda i: (i, 0)),
                  const((d, d)), const((pd, d)), const((d, d)), const((1, d))],
        out_specs=pl.BlockSpec((tm, d), lambda i: (i, 0)),
        compiler_params=_params("parallel"), name="out_stage",
    )(x, m, p, ow["w_out"], ow["w_ple"], ow["w_ple_gate"], ow["norm_final_g"])


def _tile(n, pref):
    return pref if n % pref == 0 else n


def kernel(x_prompt, x_sample, cache_k, cache_v, state_wkv, state_shift, page_table, p_prompt,
           p_sample, norm_in_g, w_in, mu_shift, w0, w2, a0, a2, k_k, k_a, r_k, lnx_g, lnx_b,
           w_rwkv_up, lambda_q1, lambda_k1, lambda_q2, lambda_k2, subln_g, w_attn_up, w_out,
           w_ple, w_ple_gate, norm_final_g):
    depth = w_in.shape[0]
    assert depth == 1, "single-layer trunk"
    layer = 0
    lam0 = 0.8 - 0.6 * math.exp(-0.3 * layer)
    bp, tp, d = x_prompt.shape
    bs, ts, _ = x_sample.shape
    assert ts == 1
    rw = w0.shape[1]
    sw = mu_shift.shape[1]
    aqk = A_HEADS * 2 * A_QK
    aw_ = A_HEADS * A_V
    row = lambda x: x.reshape(1, -1)

    w_all = w_in[layer].astype(BF16)
    c_zr, c_q, c_k, c_v, c_za = sw, sw + rw, sw + rw + aqk, sw + rw + 2 * aqk, sw + rw + 2 * aqk + aw_
    n_zag = w_all.shape[1] - c_za

    lora = w2.shape[1]
    zpad = jnp.zeros((LANES - lora, rw), BF16)
    pw = dict(mu=row(mu_shift[layer]), w0=row(w0[layer]), a0=row(a0[layer]), k_k=row(k_k[layer]),
              k_a=row(k_a[layer]), r_k=row(r_k[layer]),
              w2p=jnp.concatenate([w2[layer].astype(BF16), zpad], axis=0),
              a2p=jnp.concatenate([zpad, a2[layer].astype(BF16)], axis=0))
    slopes = 2.0 ** (-8.0 * jnp.arange(1, A_HEADS + 1, dtype=F32) / A_HEADS)
    aw = dict(slopes=slopes, slopes16=jnp.tile(slopes, 2).reshape(2 * A_HEADS, 1),
              lq1=row(lambda_q1[layer]), lk1=row(lambda_k1[layer]), lq2=row(lambda_q2[layer]),
              lk2=row(lambda_k2[layer]), subln_g=row(subln_g[layer]))
    mw = dict(lnx_g=row(lnx_g[layer]), lnx_b=row(lnx_b[layer]),
              w_r_up=w_rwkv_up[layer].astype(BF16), w_a_up=w_attn_up[layer].astype(BF16))
    ow = dict(w_out=w_out[layer].astype(BF16), w_ple=w_ple[layer].astype(BF16),
              w_ple_gate=w_ple_gate[layer].astype(BF16), norm_final_g=row(norm_final_g))

    def project(x2d, tm):
        xn = _rms_cast(x2d, norm_in_g[layer], _tile(x2d.shape[0], 512))
        sh = _matmul(xn, w_all, 0, sw, F32, tm, 640, name="proj_shift")
        zr = _matmul(xn, w_all, c_zr, rw, F32, tm, 1024, name="proj_zr")
        q = _matmul(xn, w_all, c_q, aqk, BF16, tm, 1024, scale=A_QK ** -0.5, name="proj_q")
        k = _matmul(xn, w_all, c_k, aqk, F32, tm, 1024, name="proj_k")
        v = _matmul(xn, w_all, c_v, aw_, F32, tm, 1024, name="proj_v")
        zag = _matmul(xn, w_all, c_za, n_zag, F32, tm, 1024, name="proj_gates")
        return sh, zr, q, k, v, zag

    xp = x_prompt.reshape(bp * tp, d)
    sh, zr, q, k, v, zag = project(xp, _tile(bp * tp, 1024))
    o_r, bonus, wkv_p = _rwkv_scan(sh, pw, bp, tp, _tile(tp, 1024), _tile(tp, 512))
    o_a = _attn_prompt(q, k, v, aw, bp, tp, _tile(tp, 1024), lam0)
    m = _merge(o_r, bonus, o_a, zr, zag, mw, _tile(bp * tp, 256))
    y_p = _out_stage(xp, m, p_prompt[layer].reshape(bp * tp, -1), ow, _tile(bp * tp, 256))
    y_prompt = y_p.reshape(bp, tp, d)
    k_prompt = k.reshape(1, bp, tp, A_HEADS, 2 * A_QK)
    v_prompt = v.reshape(1, bp, tp, A_HEADS, A_V)
    shift_prompt = sh.reshape(bp, tp, sw)[:, -1][None]

    xs = x_sample.reshape(bs, d)
    sh_s, zr_s, q_s, k_s, v_s, zag_s = project(xs, bs)
    r_, lw_, k_, v_, al_, be_, bonus_s = _rwkv_prep_sample(sh_s, state_shift[layer], pw)
    o_r, wkv_s = _rwkv_step(state_wkv[layer], r_, lw_, k_, v_, al_, be_)
    q_h = q_s.reshape(bs, 1, A_HEADS, 2, A_QK)
    sel = jnp.eye(2, dtype=q_s.dtype).reshape(1, 2, 1, 2, 1)
    qm = (q_h * sel).reshape(bs, 2 * A_HEADS, 2 * A_QK)
    o_a = _attn_sample(qm, k_s.reshape(bs, A_HEADS, 2 * A_QK), v_s.reshape(bs, A_HEADS, A_V),
                       cache_k[layer], cache_v[layer], page_table, aw,
                       8 if page_table.shape[1] % 8 == 0 else 1, lam0)
    m = _merge(o_r, bonus_s, o_a.reshape(bs, aw_), zr_s, zag_s, mw, bs)
    y_s = _out_stage(xs, m, p_sample[layer].reshape(bs, -1), ow, bs)
    y_sample = y_s.reshape(bs, ts, d)
    k_sample = k_s.reshape(1, bs, ts, A_HEADS, 2 * A_QK)
    v_sample = v_s.reshape(1, bs, ts, A_HEADS, A_V)

    return (y_prompt, y_sample, k_prompt, v_prompt, k_sample, v_sample,
            wkv_p[None], wkv_s[None], shift_prompt, sh_s[None])
```

```python
import collections
import functools
import math

import jax
import jax.numpy as jnp
from jax import lax
from jax.experimental import pallas as pl
from jax.experimental.pallas import tpu as pltpu

F32 = jnp.float32
BF16 = jnp.bfloat16

LANES = 128
A_HEADS = 8
A_QK = 64
A_V = 2 * A_QK
R_HEAD = 64
RMS_EPS = 1e-5
GN_EPS = 64e-5
NEG_INF = -1e30
RWKV_CHUNK = 64
VMEM_LIMIT = 56 * 1024 * 1024

NT_DIMS = (((1,), (1,)), ((), ()))
TN_DIMS = (((0,), (0,)), ((), ()))


def _params(*sem):
    return pltpu.CompilerParams(dimension_semantics=sem, vmem_limit_bytes=VMEM_LIMIT)


def _dot(a, b):
    return jnp.dot(a, b, preferred_element_type=F32)


def _dg(a, b, dims):
    return lax.dot_general(a, b, dims, preferred_element_type=F32)


def _split_dot_rhs(a_bf16, x):
    hi = x.astype(BF16)
    lo = (x - hi.astype(F32)).astype(BF16)
    return _dot(a_bf16, hi) + _dot(a_bf16, lo)


def _split_dot_lhs(x, b_bf16):
    hi = x.astype(BF16)
    lo = (x - hi.astype(F32)).astype(BF16)
    return _dot(hi, b_bf16) + _dot(lo, b_bf16)


def _sigmoid(x):
    return 1.0 / (1.0 + jnp.exp(-x))


def _silu(x):
    return x * _sigmoid(x)


def _head_group_matrix():
    i = lax.broadcasted_iota(jnp.int32, (LANES, LANES), 0)
    j = lax.broadcasted_iota(jnp.int32, (LANES, LANES), 1)
    return ((i // R_HEAD) == (j // R_HEAD)).astype(BF16)


def _head_sums(x, gmat):
    cols = x.shape[1] // LANES
    parts = [_split_dot_lhs(x[:, c * LANES:(c + 1) * LANES], gmat) for c in range(cols)]
    return jnp.concatenate(parts, axis=1)


def _rms_kernel(x_ref, g_ref, o_ref):
    x = x_ref[...]
    y = x * lax.rsqrt(jnp.mean(x * x, axis=-1, keepdims=True) + RMS_EPS)
    o_ref[...] = (y * g_ref[...]).astype(o_ref.dtype)


def _rms_cast(x, g, tm):
    m, d = x.shape
    return pl.pallas_call(
        _rms_kernel,
        out_shape=jax.ShapeDtypeStruct((m, d), BF16),
        grid=(m // tm,),
        in_specs=[pl.BlockSpec((tm, d), lambda i: (i, 0)),
                  pl.BlockSpec((1, d), lambda i: (0, 0))],
        out_specs=pl.BlockSpec((tm, d), lambda i: (i, 0)),
        compiler_params=_params("parallel"),
        name="rms_cast",
    )(x, g.reshape(1, d))


def _mm_kernel(a_ref, b_ref, o_ref, *, scale):
    acc = _dot(a_ref[...], b_ref[...])
    if scale != 1.0:
        acc = acc * scale
    o_ref[...] = acc.astype(o_ref.dtype)


def _matmul(a, b, col0, n, out_dtype, tm, tn, scale=1.0, name="matmul"):
    m, k = a.shape
    return pl.pallas_call(
        functools.partial(_mm_kernel, scale=scale),
        out_shape=jax.ShapeDtypeStruct((m, n), out_dtype),
        grid=(m // tm, n // tn),
        in_specs=[pl.BlockSpec((tm, k), lambda i, j: (i, 0)),
                  pl.BlockSpec((pl.Element(k), pl.Element(tn)),
                               lambda i, j: (0, pl.multiple_of(col0 + j * tn, LANES)))],
        out_specs=pl.BlockSpec((tm, tn), lambda i, j: (i, j)),
        compiler_params=_params("parallel", "parallel"),
        name=name,
    )(a, b)


def _rwkv_token_math(sh3, lora, prev3, prev_lora, mu3, mu_lora, w0, w2p, a0, a2p, k_k, k_a, r_k):
    lerp = lambda x, p, mu: x + mu * (p - x)
    r, kr, vr = (lerp(x, p, mu) for x, p, mu in zip(sh3, prev3, mu3))
    lo = lerp(lora, prev_lora, mu_lora)
    z = w0 + _dot(jnp.tanh(lo).astype(BF16), w2p)
    softplus = jnp.maximum(-z, 0.0) + jnp.log1p(jnp.exp(-jnp.abs(z)))
    logw = -jnp.exp(-softplus - 0.5)
    a = _sigmoid(a0 + _dot(lo.astype(BF16), a2p))
    kk = kr * k_k
    gmat = _head_group_matrix()
    norm = jnp.maximum(jnp.sqrt(_head_sums(kk * kk, gmat)), 1e-12)
    kk = kk / norm
    kmod = kr * (1.0 + (a - 1.0) * k_a)
    bonus = _head_sums(r * kmod * r_k, gmat) * vr
    return r, logw, kmod, vr, -kk, kk * a, bonus


def _rwkv_prep_sample_kernel(sh_ref, prev_ref, mu_ref, w0_ref, w2_ref, a0_ref, a2_ref,
                             kk_ref, ka_ref, rk_ref, r_o, lw_o, k_o, v_o, al_o, be_o, bo_o):
    rw = w0_ref.shape[1]
    win3 = lambda ref: tuple(ref[:, i * rw:(i + 1) * rw] for i in range(3))
    lo = lambda ref: ref[:, 3 * rw:]
    outs = _rwkv_token_math(win3(sh_ref), lo(sh_ref), win3(prev_ref), lo(prev_ref), win3(mu_ref),
                            lo(mu_ref), w0_ref[...], w2_ref[...], a0_ref[...], a2_ref[...],
                            kk_ref[...], ka_ref[...], rk_ref[...])
    for ref, val in zip((r_o, lw_o, k_o, v_o, al_o, be_o, bo_o), outs):
        ref[...] = val


def _rwkv_prep_sample(sh, prev, pw):
    rows, sw = sh.shape
    rw = pw["w0"].shape[1]
    const = lambda shape: pl.BlockSpec(shape, lambda *_: (0,) * len(shape))
    vec = const((1, rw))
    return pl.pallas_call(
        _rwkv_prep_sample_kernel, out_shape=[jax.ShapeDtypeStruct((rows, rw), F32)] * 7, grid=(1,),
        in_specs=[const((rows, sw)), const((rows, sw)), const((1, sw)), vec, const(pw["w2p"].shape), vec,
                  const(pw["a2p"].shape), vec, vec, vec],
        out_specs=[const((rows, rw))] * 7,
        compiler_params=_params("arbitrary"), name="rwkv_prep_sample",
    )(sh, prev, pw["mu"], pw["w0"], pw["w2p"], pw["a0"], pw["a2p"], pw["k_k"], pw["k_a"], pw["r_k"])


def _rwkv_chunk_kernel(pt_ref, r_sh, k_sh, v_sh, lo_sh, r_tl, k_tl, v_tl, lo_tl, mu_r, mu_k, mu_v, mu_lo,
                       w0_ref, w2_ref, a0_ref, a2_ref, kk_ref, ka_ref, rk_ref,
                       qm_ref, slope_ref, ks_ref, vs_ref, lq1, lk1, lq2, lk2, g_ref, ck_hbm, cv_hbm,
                       m_o, n_o, lr_o, op_o, wc_o, bo_o, od_o,
                       kbuf, vbuf, sem, m_scr, l_scr, acc_scr,
                       *, n_chunks, tiles_per_seq, pairs, dec):
    step = pl.program_id(0) * pairs + pl.program_id(1)
    slot = step % 2

    def page_copies(st, sl):
        seq = st // dec.sps
        first = (st % dec.sps) * dec.pps
        copies = []
        for i in range(dec.pps):
            phys = pt_ref[seq, first + i]
            copies.append(pltpu.make_async_copy(ck_hbm.at[phys], kbuf.at[sl, i], sem.at[sl, 0, i]))
            copies.append(pltpu.make_async_copy(cv_hbm.at[phys], vbuf.at[sl, i], sem.at[sl, 1, i]))
        return copies

    @pl.when(step == 0)
    def _():
        for cp in page_copies(step, slot):
            cp.start()

    @pl.when(step + 1 < pl.num_programs(0) * pairs)
    def _():
        for cp in page_copies(step + 1, 1 - slot):
            cp.start()

    @pl.when(step % dec.sps == 0)
    def _():
        m_scr[...] = jnp.full(m_scr.shape, NEG_INF, F32)
        l_scr[...] = jnp.zeros_like(l_scr)
        acc_scr[...] = jnp.zeros_like(acc_scr)

    for cp in page_copies(step, slot):
        cp.wait()

    nrow = 2 * A_HEADS
    pcols = dec.page * A_HEADS
    qm = qm_ref[...]
    dstate = {}

    def dec_scores():
        slope = slope_ref[...]
        r_i = lax.broadcasted_iota(jnp.int32, (nrow, pcols), 0)
        c_i = lax.broadcasted_iota(jnp.int32, (nrow, pcols), 1)
        valid = (c_i % A_HEADS) == (r_i % A_HEADS)
        tok = (c_i // A_HEADS).astype(F32)
        first = (step % dec.sps) * dec.pps
        scores = []
        for i in range(dec.pps):
            base = ((dec.n_pages - (first + i)) * dec.page).astype(F32)
            s = _dg(qm, kbuf[slot, i].astype(BF16), NT_DIMS) - slope * (base - tok)
            scores.append(jnp.where(valid, s, NEG_INF))
        dstate["scores"] = scores

    def dec_softmax():
        scores = dstate["scores"]
        m_old = m_scr[...]
        m_blk = functools.reduce(jnp.maximum, [jnp.max(s, axis=-1, keepdims=True) for s in scores])
        m_new = jnp.maximum(m_old, m_blk)
        corr = jnp.exp(m_old - m_new)
        probs = [jnp.exp(s - m_new) for s in scores]
        l_new = l_scr[...] * corr
        for pr in probs:
            l_new = l_new + jnp.sum(pr, axis=-1, keepdims=True)
        l_scr[...] = l_new
        m_scr[...] = m_new
        dstate["probs"] = [pr.astype(BF16) for pr in probs]
        dstate["corr"] = corr

    def dec_values():
        acc = acc_scr[...] * dstate["corr"]
        for i, pr in enumerate(dstate["probs"]):
            acc = acc + _dot(pr, vbuf[slot, i].astype(BF16))
        acc_scr[...] = acc

    seq_start = pl.program_id(0) % tiles_per_seq == 0
    trow = lax.broadcasted_iota(jnp.int32, r_sh.shape, 0)

    def prev_of(ref, tail_ref):
        first = jnp.where(seq_start, 0.0, tail_ref[7:8, :])
        return jnp.where(trow == 0, first, pltpu.roll(ref[...], 1, axis=0))

    r_all, lw_all, k_all, v_all, al_all, be_all, bonus = _rwkv_token_math(
        (r_sh[...], k_sh[...], v_sh[...]), lo_sh[...],
        (prev_of(r_sh, r_tl), prev_of(k_sh, k_tl), prev_of(v_sh, v_tl)), prev_of(lo_sh, lo_tl),
        (mu_r[...], mu_k[...], mu_v[...]), mu_lo[...], w0_ref[...], w2_ref[...], a0_ref[...],
        a2_ref[...], kk_ref[...], ka_ref[...], rk_ref[...])
    bo_o[...] = bonus

    c = RWKV_CHUNK
    two_c = 2 * c
    row = lax.broadcasted_iota(jnp.int32, (two_c, LANES), 0)
    col = lax.broadcasted_iota(jnp.int32, (two_c, LANES), 1)
    keep = (row >= c) == (col >= R_HEAD)
    strict = col < row
    incl = col <= row
    eye = (row == col).astype(F32)
    ti = lax.broadcasted_iota(jnp.int32, (c, c), 0)
    si = lax.broadcasted_iota(jnp.int32, (c, c), 1)
    ltri = (si <= ti).astype(BF16)

    def stack(x):
        return jnp.where(keep, jnp.concatenate([x, x], axis=0), 0.0).astype(BF16)

    cs = range(n_chunks)
    each = lambda fn, *lists: [fn(*xs) for xs in zip(*lists)]
    split = lambda x: [x[ci * c:(ci + 1) * c, :] for ci in cs]
    r, lw, k, v, al, be = (split(x) for x in (r_all, lw_all, k_all, v_all, al_all, be_all))
    cl = each(lambda x: _split_dot_rhs(ltri, x), lw)
    e_pos = each(jnp.exp, cl)
    e_neg = each(lambda x: jnp.exp(-x), cl)
    e_prev = each(lambda x, y: jnp.exp(x - y), cl, lw)
    wc = each(lambda x: x[c - 1:c, :], e_pos)
    kt = each(jnp.multiply, k, e_neg)
    bt = each(jnp.multiply, be, e_neg)
    la = each(lambda x, y: stack(x * y), al, e_prev)
    lr = each(lambda x, y: stack(x * y), r, e_pos)
    rb, rk, vs = each(stack, bt), each(stack, kt), each(stack, v)
    rbw = each(lambda x, y: stack(x * y), bt, wc)
    rkw = each(lambda x, y: stack(x * y), kt, wc)
    nt_dot = lambda x, y: _dg(x, y, NT_DIMS)
    a_ab = each(lambda x, y: jnp.where(strict, nt_dot(x, y), 0.0), la, rb)
    a_ak = each(lambda x, y: jnp.where(strict, nt_dot(x, y), 0.0).astype(BF16), la, rk)
    a_rb = each(lambda x, y: jnp.where(incl, nt_dot(x, y), 0.0).astype(BF16), lr, rb)
    a_rk = each(lambda x, y: jnp.where(incl, nt_dot(x, y), 0.0).astype(BF16), lr, rk)
    dec_scores()
    p = each(lambda x: eye + x, a_ab)
    pw = a_ab
    for level in range(int(math.log2(c)) - 1):
        pwb = each(lambda x: x.astype(BF16), pw)
        pw = each(_dot, pwb, pwb)
        p = each(lambda x, y: x + _dot(x.astype(BF16), y.astype(BF16)), p, pw)
        if level == 1:
            dec_softmax()
    tb = each(lambda x: x.astype(BF16), p)
    ua = each(lambda x, y: _dot(x, y).astype(BF16), tb, la)
    akv = each(lambda x, y: _dot(x, y).astype(BF16), a_ak, vs)
    uv = each(lambda x, y: _dot(x, y).astype(BF16), tb, akv)
    dec_values()
    lr2 = each(lambda x, y, z: (x.astype(F32) + _dot(y, z)).astype(BF16), lr, a_rb, ua)
    op = each(lambda x, y, z, w: _dot(x, y) + _dot(z, w), a_rb, uv, a_rk, vs)
    m2 = each(lambda x, y: _dg(x, y, TN_DIMS).astype(BF16), ua, rbw)
    n2 = each(lambda x, y, z, w: _dg(jnp.concatenate([x, y], axis=0),
                                     jnp.concatenate([z, w], axis=0), TN_DIMS), uv, vs, rbw, rkw)
    for ci in cs:
        lr_o[ci] = lr2[ci]
        op_o[pl.ds(ci * c, c), :] = op[ci][:c] + op[ci][c:]
        m_o[ci] = m2[ci]
        n_o[ci] = n2[ci]
        wc_o[ci] = wc[ci]

    @pl.when(step % dec.sps == dec.sps - 1)
    def _():
        k_self = jnp.concatenate([ks_ref[...], ks_ref[...]], axis=0).astype(BF16).astype(F32)
        v_self = jnp.concatenate([vs_ref[...], vs_ref[...]], axis=0).astype(BF16).astype(F32)
        s_self = jnp.sum(qm.astype(F32) * k_self, axis=-1, keepdims=True)
        m_run = m_scr[...]
        m_fin = jnp.maximum(m_run, s_self)
        c_fin = jnp.exp(m_run - m_fin)
        p_self = jnp.exp(s_self - m_fin)
        o = (acc_scr[...] * c_fin + p_self * v_self) / (l_scr[...] * c_fin + p_self)
        lam = _lambda(lq1[...], lk1[...], lq2[...], lk2[...], dec.lam0)
        od_o[...] = _diff_finish(o[:A_HEADS], o[A_HEADS:], lam, g_ref[...], dec.lam0)


def _rwkv_state_kernel(m_ref, n_ref, lr_ref, op_ref, wc_ref, o_ref, s_ref, s_scr, *, n_chunks, pairs):
    c = RWKV_CHUNK

    @pl.when(pl.program_id(1) == 0)
    def _():
        s_scr[...] = jnp.zeros_like(s_scr)

    def body(ci, carry):
        sl = pl.ds(pl.multiple_of(ci * c, c), c)
        outs = []
        for p in range(pairs):
            s = s_scr[p]
            sb = s.astype(BF16)
            o = _dg(lr_ref[ci, p], sb, NT_DIMS)
            outs.append(o[:c] + o[c:])
            s_scr[p] = s * wc_ref[ci, p] + _dot(sb, m_ref[ci, p]) + n_ref[ci, p]
        o_ref[sl, :] = jnp.concatenate(outs, axis=1) + op_ref[sl, :]
        return carry

    lax.fori_loop(0, n_chunks, body, 0)
    s_ref[...] = s_scr[...]


DecodePlan = collections.namedtuple("DecodePlan", "pps sps n_pages page lam0")


def _rwkv_scan(sh, pw, batch, seq_len, tt_chunk, tt_state, qm, k_self, v_self, cache_k, cache_v,
               page_table, aw, lam0):
    rows = sh.shape[0]
    rw = pw["w0"].shape[1]
    pairs = rw // LANES
    c = RWKV_CHUNK
    n_all = rows // c
    nb = tt_chunk // c
    gcols = rw // LANES
    n_steps = (rows // tt_chunk) * pairs
    dbatch, n_pages = page_table.shape
    n_phys, page = cache_k.shape[0], cache_k.shape[1]
    assert (dbatch * n_pages) % n_steps == 0, "decode pages must spread evenly over the chunk grid"
    pps = dbatch * n_pages // n_steps
    assert n_pages % pps == 0
    dec = DecodePlan(pps=pps, sps=n_pages // pps, n_pages=n_pages, page=page, lam0=lam0)
    ck = cache_k.reshape(n_phys, page * A_HEADS, LANES)
    cv = cache_v.reshape(n_phys, page * A_HEADS, LANES)
    nrow = 2 * A_HEADS

    tile = pl.BlockSpec((tt_chunk, LANES), lambda i, p, *_: (i, p))
    mat = pl.BlockSpec((nb, None, LANES, LANES), lambda i, p, *_: (i, p, 0, 0))
    win = lambda g: pl.BlockSpec((tt_chunk, LANES), lambda i, p, *_: (i, g * gcols + p))
    lora = pl.BlockSpec((tt_chunk, LANES), lambda i, p, *_: (i, 3 * gcols))
    trow = lambda i: jnp.maximum(i * (tt_chunk // 8) - 1, 0)
    twin = lambda g: pl.BlockSpec((8, LANES), lambda i, p, *_: (trow(i), g * gcols + p))
    tlora = pl.BlockSpec((8, LANES), lambda i, p, *_: (trow(i), 3 * gcols))
    mwin = lambda g: pl.BlockSpec((1, LANES), lambda i, p, *_: (0, g * gcols + p))
    mlora = pl.BlockSpec((1, LANES), lambda i, p, *_: (0, 3 * gcols))
    vec = pl.BlockSpec((1, LANES), lambda i, p, *_: (0, p))
    up = pl.BlockSpec((LANES, LANES), lambda i, p, *_: (0, p))
    per_seq = lambda r: pl.BlockSpec((None, r, LANES), lambda i, p, *_: ((i * pairs + p) // dec.sps, 0, 0))
    cvec = lambda n: pl.BlockSpec((1, n), lambda i, p, *_: (0, 0))
    hbm = pl.BlockSpec(memory_space=pl.ANY)
    grid_spec = pltpu.PrefetchScalarGridSpec(
        num_scalar_prefetch=1, grid=(rows // tt_chunk, pairs),
        in_specs=[win(0), win(1), win(2), lora, twin(0), twin(1), twin(2), tlora,
                  mwin(0), mwin(1), mwin(2), mlora, vec, up, vec, up, vec, vec, vec,
                  per_seq(nrow), pl.BlockSpec((nrow, 1), lambda i, p, *_: (0, 0)),
                  per_seq(A_HEADS), per_seq(A_HEADS),
                  cvec(A_QK), cvec(A_QK), cvec(A_QK), cvec(A_QK), cvec(A_V), hbm, hbm],
        out_specs=[mat, mat, pl.BlockSpec((nb, None, 2 * c, LANES), lambda i, p, *_: (i, p, 0, 0)), tile,
                   pl.BlockSpec((nb, None, 1, LANES), lambda i, p, *_: (i, p, 0, 0)), tile,
                   per_seq(A_HEADS)],
        scratch_shapes=[pltpu.VMEM((2, pps, page * A_HEADS, LANES), F32),
                        pltpu.VMEM((2, pps, page * A_HEADS, LANES), F32),
                        pltpu.SemaphoreType.DMA((2, 2, pps)),
                        pltpu.VMEM((nrow, 1), F32), pltpu.VMEM((nrow, 1), F32),
                        pltpu.VMEM((nrow, LANES), F32)])
    m_, n_, lr_, op_, wc_, bonus, o_dec = pl.pallas_call(
        functools.partial(_rwkv_chunk_kernel, n_chunks=nb, tiles_per_seq=seq_len // tt_chunk,
                          pairs=pairs, dec=dec),
        out_shape=[jax.ShapeDtypeStruct((n_all, pairs, LANES, LANES), BF16),
                   jax.ShapeDtypeStruct((n_all, pairs, LANES, LANES), F32),
                   jax.ShapeDtypeStruct((n_all, pairs, 2 * c, LANES), BF16),
                   jax.ShapeDtypeStruct((rows, rw), F32),
                   jax.ShapeDtypeStruct((n_all, pairs, 1, LANES), F32),
                   jax.ShapeDtypeStruct((rows, rw), F32),
                   jax.ShapeDtypeStruct((dbatch, A_HEADS, LANES), F32)],
        grid_spec=grid_spec,
        compiler_params=_params("arbitrary", "arbitrary"),
        name="rwkv_chunk_decode",
    )(page_table, sh, sh, sh, sh, sh, sh, sh, sh, pw["mu"], pw["mu"], pw["mu"], pw["mu"],
      pw["w0"], pw["w2p"], pw["a0"], pw["a2p"], pw["k_k"], pw["k_a"], pw["r_k"],
      qm, aw["slopes16"], k_self, v_self, aw["lq1"], aw["lk1"], aw["lq2"], aw["lk2"], aw["subln_g"],
      ck, cv)

    nt = seq_len // tt_state
    ns = tt_state // c
    smat = lambda rws: pl.BlockSpec((ns, pairs, rws, LANES), lambda b, i: (b * nt + i, 0, 0, 0))
    wide = pl.BlockSpec((tt_state, rw), lambda b, i: (b * nt + i, 0))
    o, s = pl.pallas_call(
        functools.partial(_rwkv_state_kernel, n_chunks=ns, pairs=pairs),
        out_shape=[jax.ShapeDtypeStruct((rows, rw), F32),
                   jax.ShapeDtypeStruct((batch, pairs, LANES, LANES), F32)],
        grid=(batch, nt),
        in_specs=[smat(LANES), smat(LANES), smat(2 * c), wide, smat(1)],
        out_specs=[wide, pl.BlockSpec((None, pairs, LANES, LANES), lambda b, i: (b, 0, 0, 0))],
        scratch_shapes=[pltpu.VMEM((pairs, LANES, LANES), F32)],
        compiler_params=_params("parallel", "arbitrary"),
        name="rwkv_state",
    )(m_, n_, lr_, op_, wc_)
    s = s.reshape(batch, pairs, 2, R_HEAD, 2, R_HEAD)
    s = jnp.stack([s[:, :, 0, :, 0, :], s[:, :, 1, :, 1, :]], axis=2)
    return o, bonus, s.reshape(batch, 2 * pairs, R_HEAD, R_HEAD), o_dec


def _rwkv_step_kernel(s_ref, r_ref, lw_ref, k_ref, al_ref, be_ref, vcol_ref, o_ref, so_ref):
    s = s_ref[...]
    sa = jnp.sum(s * al_ref[...], axis=-1, keepdims=True)
    s_new = s * jnp.exp(lw_ref[...]) + sa * be_ref[...] + vcol_ref[...] * k_ref[...]
    so_ref[...] = s_new
    o_ref[...] = jnp.sum(s_new * r_ref[...], axis=-1, keepdims=True)


def _rwkv_step(state, r, lw, k, v, al, be):
    b, h, n, _ = state.shape
    vec = lambda x: x.reshape(b, h, 1, n)
    vspec = pl.BlockSpec((None, h, 1, n), lambda i: (i, 0, 0, 0))
    cspec = pl.BlockSpec((None, h, n, 1), lambda i: (i, 0, 0, 0))
    sspec = pl.BlockSpec((None, h, n, n), lambda i: (i, 0, 0, 0))
    o, s_new = pl.pallas_call(
        _rwkv_step_kernel,
        out_shape=[jax.ShapeDtypeStruct((b, h, n, 1), F32), jax.ShapeDtypeStruct(state.shape, F32)],
        grid=(b,),
        in_specs=[sspec, vspec, vspec, vspec, vspec, vspec, cspec],
        out_specs=[cspec, sspec],
        compiler_params=_params("parallel"), name="rwkv_step",
    )(state, vec(r), vec(lw), vec(k), vec(al), vec(be), v.reshape(b, h, n, 1))
    return o.reshape(b, h * n), s_new


def _lambda(lq1, lk1, lq2, lk2, lam0):
    t1 = jnp.sum(lq1 * lk1, axis=-1, keepdims=True)
    t2 = jnp.sum(lq2 * lk2, axis=-1, keepdims=True)
    return jnp.exp(t1) - jnp.exp(t2) + lam0


def _diff_finish(o1, o2, lam, subln_g, lam0):
    o = o1 - lam * o2
    y = o * lax.rsqrt(jnp.mean(o * o, axis=-1, keepdims=True) + RMS_EPS)
    return y * subln_g * (1.0 - lam0)


ATTN_GROUP = 256
ATTN_KEYS = 512
ATTN_AHEAD = 2
ALIBI_SPLIT = 256


def _attn_prompt_kernel(slopes_ref, q_ref, k_ref, v_ref, lq1, lk1, lq2, lk2, g_ref, o_ref,
                        kb, vt, *, tq, lam0):
    h = pl.program_id(1)
    qi = pl.program_id(2)
    seq_len = k_ref.shape[0]

    @pl.when(qi == 0)
    def _():
        slope = slopes_ref[h]
        pos = lax.broadcasted_iota(jnp.int32, (seq_len, LANES), 0)
        lane = lax.broadcasted_iota(jnp.int32, (seq_len, LANES), 1)
        lo = (pos % ALIBI_SPLIT).astype(F32) * slope
        hi = (pos - pos % ALIBI_SPLIT).astype(F32) * slope
        kb[:, :LANES] = k_ref[...].astype(BF16)
        kb[:, LANES:] = jnp.where(lane == 0, lo, jnp.where(lane == 1, hi, 0.0)).astype(BF16)
        for c0 in range(0, seq_len, tq):
            vt[:, c0:c0 + tq] = v_ref[c0:c0 + tq, :].T.astype(BF16)

    lane = lax.broadcasted_iota(jnp.int32, (tq, LANES), 1)
    q = q_ref[...]
    ones = jnp.where(lane < 2, 1.0, 0.0).astype(BF16)
    zero = jnp.zeros_like(q)
    q_aug = jnp.concatenate([
        jnp.concatenate([jnp.where(lane < A_QK, q, zero), ones], axis=1),
        jnp.concatenate([jnp.where(lane >= A_QK, q, zero), ones], axis=1)], axis=0)
    gw = ATTN_GROUP
    groups = range(2 * tq // gw)
    q_grp = [q_aug[g * gw:(g + 1) * gw, :] for g in groups]

    ksub = min(ATTN_KEYS, tq)

    def block(j, carry, diagonal):
        m, l, acc = (list(x) for x in carry)
        steps = []
        for a in range(tq // ksub):
            for g in groups:
                q0 = (g * gw) % tq
                rows = min(ksub, q0 + gw - a * ksub) if diagonal else ksub
                if rows > 0:
                    steps.append((a, g, rows, diagonal and (a * ksub + rows - 1 > q0)))

        def scores(a, g, rows, masked):
            ks = pl.ds(pl.multiple_of(j * tq + a * ksub, ksub), rows)
            s = _dg(kb[ks, :], q_grp[g], NT_DIMS)
            if masked:
                r_i = lax.broadcasted_iota(jnp.int32, (rows, gw), 0) + a * ksub
                c_i = lax.broadcasted_iota(jnp.int32, (rows, gw), 1) + (g * gw) % tq
                s = jnp.where(r_i <= c_i, s, NEG_INF)
            return s

        ahead = [scores(*st) for st in steps[:ATTN_AHEAD]]
        for idx, (a, g, rows, _) in enumerate(steps):
            if idx + ATTN_AHEAD < len(steps):
                ahead.append(scores(*steps[idx + ATTN_AHEAD]))
            s = ahead.pop(0)
            ks = pl.ds(pl.multiple_of(j * tq + a * ksub, ksub), rows)
            m_new = jnp.maximum(m[g], jnp.max(s, axis=0, keepdims=True))
            p = jnp.exp(s - m_new)
            corr = jnp.exp(m[g] - m_new)
            l[g] = l[g] * corr + jnp.sum(p, axis=0, keepdims=True)
            acc[g] = acc[g] * corr + _dot(vt[:, ks], p.astype(BF16))
            m[g] = m_new
        return tuple(m), tuple(l), tuple(acc)

    init = (tuple(jnp.full((1, gw), NEG_INF, F32) for _ in groups),
            tuple(jnp.zeros((1, gw), F32) for _ in groups),
            tuple(jnp.zeros((LANES, gw), F32) for _ in groups))
    carry = lax.fori_loop(0, qi, lambda j, cr: block(j, cr, False), init)
    _, l, acc = block(qi, carry, True)

    o = jnp.concatenate([a / x for a, x in zip(acc, l)], axis=1).T
    lam = _lambda(lq1[...], lk1[...], lq2[...], lk2[...], lam0)
    o_ref[...] = _diff_finish(o[:tq], o[tq:], lam, g_ref[...], lam0)


def _attn_prompt(q, k, v, aw, batch, seq_len, tq, lam0):
    rows, width = k.shape
    nq = seq_len // tq
    qspec = pl.BlockSpec((tq, LANES), lambda b, h, i, *_: (b * nq + i, h))
    kvspec = pl.BlockSpec((seq_len, LANES), lambda b, h, i, *_: (b, h))
    vec = lambda n: pl.BlockSpec((1, n), lambda b, h, i, *_: (0, 0))
    grid_spec = pltpu.PrefetchScalarGridSpec(
        num_scalar_prefetch=1, grid=(batch, A_HEADS, nq),
        in_specs=[qspec, kvspec, kvspec, vec(A_QK), vec(A_QK), vec(A_QK), vec(A_QK), vec(A_V)],
        out_specs=qspec,
        scratch_shapes=[pltpu.VMEM((seq_len, 2 * LANES), BF16), pltpu.VMEM((LANES, seq_len), BF16)])
    return pl.pallas_call(
        functools.partial(_attn_prompt_kernel, tq=tq, lam0=lam0),
        out_shape=jax.ShapeDtypeStruct((rows, width), F32),
        grid_spec=grid_spec,
        compiler_params=_params("parallel", "parallel", "arbitrary"),
        name="attn_prompt",
    )(aw["slopes"], q, k, v, aw["lq1"], aw["lk1"], aw["lq2"], aw["lk2"], aw["subln_g"])


def _merge_kernel(or_ref, bo_ref, zr_ref, oa_ref, za_ref, ga_ref, gr_ref,
                  lng_ref, lnb_ref, wr_ref, wa_ref, o_ref):
    gmat = _head_group_matrix()
    o_r = or_ref[...]
    inv_n = 1.0 / R_HEAD
    mean = _head_sums(o_r, gmat) * inv_n
    d = o_r - mean
    var = _head_sums(d * d, gmat) * inv_n
    o_r = d * lax.rsqrt(var + GN_EPS) * lng_ref[...] + lnb_ref[...] + bo_ref[...]
    x_r = (o_r * _silu(zr_ref[...])).astype(BF16)
    x_a = (oa_ref[...] * _silu(za_ref[...])).astype(BF16)
    y_r = _dot(x_r, wr_ref[...])
    y_a = _dot(x_a, wa_ref[...])
    o_ref[...] = (_sigmoid(ga_ref[...]) * y_a + _sigmoid(gr_ref[...]) * y_r).astype(o_ref.dtype)


def _merge(o_r, bonus, o_a, zr, zag, mw, tm):
    rows, rw = o_r.shape
    d = mw["w_r_up"].shape[1]
    tile = pl.BlockSpec((tm, rw), lambda i: (i, 0))
    gate = lambda col0: pl.BlockSpec((pl.Element(tm), pl.Element(d)),
                                     lambda i: (pl.multiple_of(i * tm, 8), col0))
    const = lambda shape: pl.BlockSpec(shape, lambda i: (0,) * len(shape))
    return pl.pallas_call(
        _merge_kernel,
        out_shape=jax.ShapeDtypeStruct((rows, d), BF16),
        grid=(rows // tm,),
        in_specs=[tile, tile, tile, tile, tile, gate(rw), gate(rw + d),
                  const((1, rw)), const((1, rw)),
                  const(mw["w_r_up"].shape), const(mw["w_a_up"].shape)],
        out_specs=pl.BlockSpec((tm, d), lambda i: (i, 0)),
        compiler_params=_params("parallel"), name="merge",
    )(o_r, bonus, zr, o_a, zag, zag, zag, mw["lnx_g"], mw["lnx_b"], mw["w_r_up"], mw["w_a_up"])


def _out_kernel(x_ref, m_ref, p_ref, wo_ref, wp_ref, wg_ref, g_ref, o_ref):
    h = x_ref[...] + _dot(m_ref[...], wo_ref[...])
    ple = _dot(p_ref[...].astype(BF16), wp_ref[...])
    h = h + ple * _sigmoid(_dot(h.astype(BF16), wg_ref[...]))
    y = h * lax.rsqrt(jnp.mean(h * h, axis=-1, keepdims=True) + RMS_EPS)
    o_ref[...] = y * g_ref[...]


def _out_stage(x, m, p, ow, tm):
    rows, d = x.shape
    pd = p.shape[1]
    const = lambda shape: pl.BlockSpec(shape, lambda i: (0,) * len(shape),
                                       pipeline_mode=pl.Buffered(1))
    return pl.pallas_call(
        _out_kernel,
        out_shape=jax.ShapeDtypeStruct((rows, d), F32),
        grid=(rows // tm,),
        in_specs=[pl.BlockSpec((tm, d), lambda i: (i, 0)), pl.BlockSpec((tm, d), lambda i: (i, 0)),
                  pl.BlockSpec((tm, pd), lambda i: (i, 0)),
                  const((d, d)), const((pd, d)), const((d, d)), const((1, d))],
        out_specs=pl.BlockSpec((tm, d), lambda i: (i, 0)),
        compiler_params=_params("parallel"), name="out_stage",
    )(x, m, p, ow["w_out"], ow["w_ple"], ow["w_ple_gate"], ow["norm_final_g"])


def _tile(n, pref):
    return pref if n % pref == 0 else n


def kernel(x_prompt, x_sample, cache_k, cache_v, state_wkv, state_shift, page_table, p_prompt,
           p_sample, norm_in_g, w_in, mu_shift, w0, w2, a0, a2, k_k, k_a, r_k, lnx_g, lnx_b,
           w_rwkv_up, lambda_q1, lambda_k1, lambda_q2, lambda_k2, subln_g, w_attn_up, w_out,
           w_ple, w_ple_gate, norm_final_g):
    depth = w_in.shape[0]
    assert depth == 1, "single-layer trunk"
    layer = 0
    lam0 = 0.8 - 0.6 * math.exp(-0.3 * layer)
    bp, tp, d = x_prompt.shape
    bs, ts, _ = x_sample.shape
    assert ts == 1
    rw = w0.shape[1]
    sw = mu_shift.shape[1]
    aqk = A_HEADS * 2 * A_QK
    aw_ = A_HEADS * A_V
    row = lambda x: x.reshape(1, -1)

    w_all = w_in[layer].astype(BF16)
    c_zr, c_q, c_k, c_v, c_za = sw, sw + rw, sw + rw + aqk, sw + rw + 2 * aqk, sw + rw + 2 * aqk + aw_
    n_zag = w_all.shape[1] - c_za

    lora = w2.shape[1]
    zpad = jnp.zeros((LANES - lora, rw), BF16)
    pw = dict(mu=row(mu_shift[layer]), w0=row(w0[layer]), a0=row(a0[layer]), k_k=row(k_k[layer]),
              k_a=row(k_a[layer]), r_k=row(r_k[layer]),
              w2p=jnp.concatenate([w2[layer].astype(BF16), zpad], axis=0),
              a2p=jnp.concatenate([zpad, a2[layer].astype(BF16)], axis=0))
    slopes = 2.0 ** (-8.0 * jnp.arange(1, A_HEADS + 1, dtype=F32) / A_HEADS)
    aw = dict(slopes=slopes, slopes16=jnp.tile(slopes, 2).reshape(2 * A_HEADS, 1),
              lq1=row(lambda_q1[layer]), lk1=row(lambda_k1[layer]), lq2=row(lambda_q2[layer]),
              lk2=row(lambda_k2[layer]), subln_g=row(subln_g[layer]))
    mw = dict(lnx_g=row(lnx_g[layer]), lnx_b=row(lnx_b[layer]),
              w_r_up=w_rwkv_up[layer].astype(BF16), w_a_up=w_attn_up[layer].astype(BF16))
    ow = dict(w_out=w_out[layer].astype(BF16), w_ple=w_ple[layer].astype(BF16),
              w_ple_gate=w_ple_gate[layer].astype(BF16), norm_final_g=row(norm_final_g))

    def project(x2d, tm):
        xn = _rms_cast(x2d, norm_in_g[layer], _tile(x2d.shape[0], 512))
        sh = _matmul(xn, w_all, 0, sw, F32, tm, 640, name="proj_shift")
        zr = _matmul(xn, w_all, c_zr, rw, F32, tm, 1024, name="proj_zr")
        q = _matmul(xn, w_all, c_q, aqk, BF16, tm, 1024, scale=A_QK ** -0.5, name="proj_q")
        k = _matmul(xn, w_all, c_k, aqk, F32, tm, 1024, name="proj_k")
        v = _matmul(xn, w_all, c_v, aw_, F32, tm, 1024, name="proj_v")
        zag = _matmul(xn, w_all, c_za, n_zag, F32, tm, 1024, name="proj_gates")
        return sh, zr, q, k, v, zag

    xp = x_prompt.reshape(bp * tp, d)
    xs = x_sample.reshape(bs, d)
    sh, zr, q, k, v, zag = project(xp, _tile(bp * tp, 1024))
    sh_s, zr_s, q_s, k_s, v_s, zag_s = project(xs, bs)
    q_h = q_s.reshape(bs, 1, A_HEADS, 2, A_QK)
    sel = jnp.eye(2, dtype=q_s.dtype).reshape(1, 2, 1, 2, 1)
    qm = (q_h * sel).reshape(bs, 2 * A_HEADS, 2 * A_QK)

    o_r, bonus, wkv_p, o_dec = _rwkv_scan(
        sh, pw, bp, tp, _tile(tp, 512), _tile(tp, 512), qm, k_s.reshape(bs, A_HEADS, 2 * A_QK),
        v_s.reshape(bs, A_HEADS, A_V), cache_k[layer], cache_v[layer], page_table, aw, lam0)
    o_a = _attn_prompt(q, k, v, aw, bp, tp, _tile(tp, 1024), lam0)
    m = _merge(o_r, bonus, o_a, zr, zag, mw, _tile(bp * tp, 256))
    y_p = _out_stage(xp, m, p_prompt[layer].reshape(bp * tp, -1), ow, _tile(bp * tp, 256))
    y_prompt = y_p.reshape(bp, tp, d)
    k_prompt = k.reshape(1, bp, tp, A_HEADS, 2 * A_QK)
    v_prompt = v.reshape(1, bp, tp, A_HEADS, A_V)
    shift_prompt = sh.reshape(bp, tp, sw)[:, -1][None]

    r_, lw_, k_, v_, al_, be_, bonus_s = _rwkv_prep_sample(sh_s, state_shift[layer], pw)
    o_r, wkv_s = _rwkv_step(state_wkv[layer], r_, lw_, k_, v_, al_, be_)
    m = _merge(o_r, bonus_s, o_dec.reshape(bs, aw_), zr_s, zag_s, mw, bs)
    y_s = _out_stage(xs, m, p_sample[layer].reshape(bs, -1), ow, bs)
    y_sample = y_s.reshape(bs, ts, d)
    k_sample = k_s.reshape(1, bs, ts, A_HEADS, 2 * A_QK)
    v_sample = v_s.reshape(1, bs, ts, A_HEADS, A_V)

    return (y_prompt, y_sample, k_prompt, v_prompt, k_sample, v_sample,
            wkv_p[None], wkv_s[None], shift_prompt, sh_s[None])
```

```python
import collections
import functools
import math
import struct

import jax
import jax.numpy as jnp
from jax import lax
from jax.experimental import pallas as pl
from jax.experimental.pallas import tpu as pltpu

F32 = jnp.float32
BF16 = jnp.bfloat16

LANES = 128
A_HEADS = 8
A_QK = 64
A_V = 2 * A_QK
R_HEAD = 64
RMS_EPS = 1e-5
GN_EPS = 64e-5
NEG_INF = -1e30
RWKV_CHUNK = 64
VMEM_LIMIT = 56 * 1024 * 1024


def _f32(x):
    return struct.unpack("<f", struct.pack("<f", x))[0]


def _bf16_terms(x):
    terms, rest = [], _f32(x)
    for _ in range(3):
        bits = struct.unpack("<I", struct.pack("<f", rest))[0] & 0xFFFF0000
        terms.append(struct.unpack("<f", struct.pack("<I", bits))[0])
        rest = _f32(rest - terms[-1])
    assert rest == 0.0
    return terms


LOG2E = _f32(1.0 / math.log(2.0))
LOG2E_TERMS = _bf16_terms(LOG2E)

NT_DIMS = (((1,), (1,)), ((), ()))
TN_DIMS = (((0,), (0,)), ((), ()))


def _params(*sem):
    return pltpu.CompilerParams(dimension_semantics=sem, vmem_limit_bytes=VMEM_LIMIT)


def _dot(a, b):
    return jnp.dot(a, b, preferred_element_type=F32)


def _dg(a, b, dims):
    return lax.dot_general(a, b, dims, preferred_element_type=F32)


def _split_dot_rhs(a_bf16, x):
    hi = x.astype(BF16)
    lo = (x - hi.astype(F32)).astype(BF16)
    return _dot(a_bf16, hi) + _dot(a_bf16, lo)


def _sigmoid(x):
    return 1.0 / (1.0 + jnp.exp(-x))


def _silu(x):
    return x * _sigmoid(x)


def _head_sums(x):
    width = x.shape[1]
    win = 2 * LANES if width % (2 * LANES) == 0 else LANES
    i = lax.broadcasted_iota(jnp.int32, (win, win), 0)
    j = lax.broadcasted_iota(jnp.int32, (win, win), 1)
    gmat = ((i // R_HEAD) == (j // R_HEAD)).astype(BF16)
    hi = x.astype(BF16)
    lo = (x - hi.astype(F32)).astype(BF16)
    parts = []
    for c0 in range(0, width, win):
        h, l = hi[:, c0:c0 + win], lo[:, c0:c0 + win]
        if win == LANES:
            parts.append(_dot(jnp.concatenate([h, l], axis=1), jnp.concatenate([gmat, gmat], axis=0)))
        else:
            parts.append(_dot(h, gmat) + _dot(l, gmat))
    return jnp.concatenate(parts, axis=1)


def _rms_kernel(x_ref, g_ref, o_ref):
    x = x_ref[...]
    y = x * lax.rsqrt(jnp.mean(x * x, axis=-1, keepdims=True) + RMS_EPS)
    o_ref[...] = (y * g_ref[...]).astype(o_ref.dtype)


def _rms_cast(x, g, tm):
    m, d = x.shape
    return pl.pallas_call(
        _rms_kernel,
        out_shape=jax.ShapeDtypeStruct((m, d), BF16),
        grid=(m // tm,),
        in_specs=[pl.BlockSpec((tm, d), lambda i: (i, 0)),
                  pl.BlockSpec((1, d), lambda i: (0, 0))],
        out_specs=pl.BlockSpec((tm, d), lambda i: (i, 0)),
        compiler_params=_params("parallel"),
        name="rms_cast",
    )(x, g.reshape(1, d))


def _mm_kernel(a_ref, b_ref, o_ref, *, scale):
    acc = _dot(a_ref[...], b_ref[...])
    if scale != 1.0:
        acc = acc * scale
    o_ref[...] = acc.astype(o_ref.dtype)


def _matmul(a, b, col0, n, out_dtype, tm, tn, scale=1.0, name="matmul"):
    m, k = a.shape
    return pl.pallas_call(
        functools.partial(_mm_kernel, scale=scale),
        out_shape=jax.ShapeDtypeStruct((m, n), out_dtype),
        grid=(m // tm, n // tn),
        in_specs=[pl.BlockSpec((tm, k), lambda i, j: (i, 0)),
                  pl.BlockSpec((pl.Element(k), pl.Element(tn)),
                               lambda i, j: (0, pl.multiple_of(col0 + j * tn, LANES)))],
        out_specs=pl.BlockSpec((tm, tn), lambda i, j: (i, j)),
        compiler_params=_params("parallel", "parallel"),
        name=name,
    )(a, b)


def _rwkv_token_math(sh3, lora, prev3, prev_lora, mu3, mu_lora, w0, w2p, a0, a2p, k_k, k_a, r_k):
    lerp = lambda x, p, mu: x + mu * (p - x)
    r, kr, vr = (lerp(x, p, mu) for x, p, mu in zip(sh3, prev3, mu3))
    lo = lerp(lora, prev_lora, mu_lora)
    z = w0 + _dot(jnp.tanh(lo).astype(BF16), w2p)
    softplus = jnp.maximum(-z, 0.0) + jnp.log1p(jnp.exp(-jnp.abs(z)))
    logw = -jnp.exp(-softplus - 0.5)
    a = _sigmoid(a0 + _dot(lo.astype(BF16), a2p))
    kk = kr * k_k
    norm = jnp.maximum(jnp.sqrt(_head_sums(kk * kk)), 1e-12)
    kk = kk / norm
    kmod = kr * (1.0 + (a - 1.0) * k_a)
    bonus = _head_sums(r * kmod * r_k) * vr
    return r, logw, kmod, vr, -kk, kk * a, bonus


def _rwkv_prep_sample_kernel(sh_ref, prev_ref, mu_ref, w0_ref, w2_ref, a0_ref, a2_ref,
                             kk_ref, ka_ref, rk_ref, r_o, lw_o, k_o, v_o, al_o, be_o, bo_o):
    rw = w0_ref.shape[1]
    win3 = lambda ref: tuple(ref[:, i * rw:(i + 1) * rw] for i in range(3))
    lo = lambda ref: ref[:, 3 * rw:]
    outs = _rwkv_token_math(win3(sh_ref), lo(sh_ref), win3(prev_ref), lo(prev_ref), win3(mu_ref),
                            lo(mu_ref), w0_ref[...], w2_ref[...], a0_ref[...], a2_ref[...],
                            kk_ref[...], ka_ref[...], rk_ref[...])
    for ref, val in zip((r_o, lw_o, k_o, v_o, al_o, be_o, bo_o), outs):
        ref[...] = val


def _rwkv_prep_sample(sh, prev, pw):
    rows, sw = sh.shape
    rw = pw["w0"].shape[1]
    const = lambda shape: pl.BlockSpec(shape, lambda *_: (0,) * len(shape))
    vec = const((1, rw))
    return pl.pallas_call(
        _rwkv_prep_sample_kernel, out_shape=[jax.ShapeDtypeStruct((rows, rw), F32)] * 7, grid=(1,),
        in_specs=[const((rows, sw)), const((rows, sw)), const((1, sw)), vec, const(pw["w2p"].shape), vec,
                  const(pw["a2p"].shape), vec, vec, vec],
        out_specs=[const((rows, rw))] * 7,
        compiler_params=_params("arbitrary"), name="rwkv_prep_sample",
    )(sh, prev, pw["mu"], pw["w0"], pw["w2p"], pw["a0"], pw["a2p"], pw["k_k"], pw["k_a"], pw["r_k"])


def _rwkv_chunk_kernel(pt_ref, r_sh, k_sh, v_sh, lo_sh, r_tl, k_tl, v_tl, lo_tl, mu_r, mu_k, mu_v, mu_lo,
                       w0_ref, w2_ref, a0_ref, a2_ref, kk_ref, ka_ref, rk_ref,
                       qm_ref, slope_ref, ks_ref, vs_ref, lq1, lk1, lq2, lk2, g_ref, ck_hbm, cv_hbm,
                       m_o, n_o, lr_o, op_o, wc_o, bo_o, od_o,
                       kbuf, vbuf, sem, m_scr, l_scr, acc_scr,
                       *, n_chunks, tiles_per_seq, pairs, dec):
    step = pl.program_id(0) * pairs + pl.program_id(1)
    slot = step % 2

    def page_copies(st, sl):
        seq = st // dec.sps
        first = (st % dec.sps) * dec.pps
        copies = []
        for i in range(dec.pps):
            phys = pt_ref[seq, first + i]
            copies.append(pltpu.make_async_copy(ck_hbm.at[phys], kbuf.at[sl, i], sem.at[sl, 0, i]))
            copies.append(pltpu.make_async_copy(cv_hbm.at[phys], vbuf.at[sl, i], sem.at[sl, 1, i]))
        return copies

    @pl.when(step == 0)
    def _():
        for cp in page_copies(step, slot):
            cp.start()

    @pl.when(step + 1 < pl.num_programs(0) * pairs)
    def _():
        for cp in page_copies(step + 1, 1 - slot):
            cp.start()

    @pl.when(step % dec.sps == 0)
    def _():
        m_scr[...] = jnp.full(m_scr.shape, NEG_INF, F32)
        l_scr[...] = jnp.zeros_like(l_scr)
        acc_scr[...] = jnp.zeros_like(acc_scr)

    for cp in page_copies(step, slot):
        cp.wait()

    nrow = 2 * A_HEADS
    pcols = dec.page * A_HEADS
    qm = qm_ref[...]
    dstate = {}

    def dec_scores():
        slope = slope_ref[...] * LOG2E
        r_i = lax.broadcasted_iota(jnp.int32, (nrow, pcols), 0)
        c_i = lax.broadcasted_iota(jnp.int32, (nrow, pcols), 1)
        valid = (c_i % A_HEADS) == (r_i % A_HEADS)
        tok = (c_i // A_HEADS).astype(F32)
        first = (step % dec.sps) * dec.pps
        scores = []
        for i in range(dec.pps):
            base = ((dec.n_pages - (first + i)) * dec.page).astype(F32)
            s = _dg(qm, kbuf[slot, i].astype(BF16), NT_DIMS) - slope * (base - tok)
            scores.append(jnp.where(valid, s, NEG_INF))
        dstate["scores"] = scores

    def dec_softmax():
        scores = dstate["scores"]
        m_old = m_scr[...]
        m_blk = functools.reduce(jnp.maximum, [jnp.max(s, axis=-1, keepdims=True) for s in scores])
        m_new = jnp.maximum(m_old, m_blk)
        corr = jnp.exp2(m_old - m_new)
        probs = [jnp.exp2(s - m_new) for s in scores]
        l_new = l_scr[...] * corr
        for pr in probs:
            l_new = l_new + jnp.sum(pr, axis=-1, keepdims=True)
        l_scr[...] = l_new
        m_scr[...] = m_new
        dstate["probs"] = [pr.astype(BF16) for pr in probs]
        dstate["corr"] = corr

    def dec_values():
        acc = acc_scr[...] * dstate["corr"]
        for i, pr in enumerate(dstate["probs"]):
            acc = acc + _dot(pr, vbuf[slot, i].astype(BF16))
        acc_scr[...] = acc

    seq_start = pl.program_id(0) % tiles_per_seq == 0
    trow = lax.broadcasted_iota(jnp.int32, r_sh.shape, 0)

    def prev_of(ref, tail_ref):
        first = jnp.where(seq_start, 0.0, tail_ref[7:8, :])
        return jnp.where(trow == 0, first, pltpu.roll(ref[...], 1, axis=0))

    r_all, lw_all, k_all, v_all, al_all, be_all, bonus = _rwkv_token_math(
        (r_sh[...], k_sh[...], v_sh[...]), lo_sh[...],
        (prev_of(r_sh, r_tl), prev_of(k_sh, k_tl), prev_of(v_sh, v_tl)), prev_of(lo_sh, lo_tl),
        (mu_r[...], mu_k[...], mu_v[...]), mu_lo[...], w0_ref[...], w2_ref[...], a0_ref[...],
        a2_ref[...], kk_ref[...], ka_ref[...], rk_ref[...])
    bo_o[...] = bonus

    c = RWKV_CHUNK
    two_c = 2 * c
    row = lax.broadcasted_iota(jnp.int32, (two_c, LANES), 0)
    col = lax.broadcasted_iota(jnp.int32, (two_c, LANES), 1)
    keep = (row >= c) == (col >= R_HEAD)
    strict = col < row
    incl = col <= row
    eye = (row == col).astype(F32)
    ti = lax.broadcasted_iota(jnp.int32, (c, c), 0)
    si = lax.broadcasted_iota(jnp.int32, (c, c), 1)
    ltri = (si <= ti).astype(BF16)

    def stack(x):
        return jnp.where(keep, jnp.concatenate([x, x], axis=0), 0.0).astype(BF16)

    cs = range(n_chunks)
    each = lambda fn, *lists: [fn(*xs) for xs in zip(*lists)]
    split = lambda x: [x[ci * c:(ci + 1) * c, :] for ci in cs]
    r, lw, k, v, al, be = (split(x) for x in (r_all, lw_all, k_all, v_all, al_all, be_all))
    cl = each(lambda x: _split_dot_rhs(ltri, x), lw)
    e_pos = each(jnp.exp, cl)
    e_neg = each(lambda x: jnp.exp(-x), cl)
    e_prev = each(lambda x, y: jnp.exp(x - y), cl, lw)
    wc = each(lambda x: x[c - 1:c, :], e_pos)
    kt = each(jnp.multiply, k, e_neg)
    bt = each(jnp.multiply, be, e_neg)
    la = each(lambda x, y: stack(x * y), al, e_prev)
    lr = each(lambda x, y: stack(x * y), r, e_pos)
    rb, rk, vs = each(stack, bt), each(stack, kt), each(stack, v)
    rbw = each(lambda x, y: stack(x * y), bt, wc)
    rkw = each(lambda x, y: stack(x * y), kt, wc)
    nt_dot = lambda x, y: _dg(x, y, NT_DIMS)
    a_ab = each(lambda x, y: jnp.where(strict, nt_dot(x, y), 0.0), la, rb)
    a_ak = each(lambda x, y: jnp.where(strict, nt_dot(x, y), 0.0).astype(BF16), la, rk)
    a_rb = each(lambda x, y: jnp.where(incl, nt_dot(x, y), 0.0).astype(BF16), lr, rb)
    a_rk = each(lambda x, y: jnp.where(incl, nt_dot(x, y), 0.0).astype(BF16), lr, rk)
    dec_scores()
    p = each(lambda x: eye + x, a_ab)
    pw = a_ab
    for level in range(int(math.log2(c)) - 1):
        pwb = each(lambda x: x.astype(BF16), pw)
        pw = each(_dot, pwb, pwb)
        p = each(lambda x, y: x + _dot(x.astype(BF16), y.astype(BF16)), p, pw)
        if level == 1:
            dec_softmax()
    tb = each(lambda x: x.astype(BF16), p)
    ua = each(lambda x, y: _dot(x, y).astype(BF16), tb, la)
    akv = each(lambda x, y: _dot(x, y).astype(BF16), a_ak, vs)
    uv = each(lambda x, y: _dot(x, y).astype(BF16), tb, akv)
    dec_values()
    lr2 = each(lambda x, y, z: (x.astype(F32) + _dot(y, z)).astype(BF16), lr, a_rb, ua)
    op = each(lambda x, y, z, w: _dot(x, y) + _dot(z, w), a_rb, uv, a_rk, vs)
    m2 = each(lambda x, y: _dg(x, y, TN_DIMS).astype(BF16), ua, rbw)
    n2 = each(lambda x, y, z, w: _dg(jnp.concatenate([x, y], axis=0),
                                     jnp.concatenate([z, w], axis=0), TN_DIMS), uv, vs, rbw, rkw)
    for ci in cs:
        lr_o[ci] = lr2[ci]
        op_o[pl.ds(ci * c, c), :] = op[ci][:c] + op[ci][c:]
        m_o[ci] = m2[ci]
        n_o[ci] = n2[ci]
        wc_o[ci] = wc[ci]

    @pl.when(step % dec.sps == dec.sps - 1)
    def _():
        k_self = jnp.concatenate([ks_ref[...], ks_ref[...]], axis=0).astype(BF16).astype(F32)
        v_self = jnp.concatenate([vs_ref[...], vs_ref[...]], axis=0).astype(BF16).astype(F32)
        s_self = jnp.sum(qm.astype(F32) * k_self, axis=-1, keepdims=True)
        m_run = m_scr[...]
        m_fin = jnp.maximum(m_run, s_self)
        c_fin = jnp.exp2(m_run - m_fin)
        p_self = jnp.exp2(s_self - m_fin)
        o = (acc_scr[...] * c_fin + p_self * v_self) / (l_scr[...] * c_fin + p_self)
        lam = _lambda(lq1[...], lk1[...], lq2[...], lk2[...], dec.lam0)
        od_o[...] = _diff_finish(o[:A_HEADS], o[A_HEADS:], lam, g_ref[...], dec.lam0)


def _rwkv_state_kernel(m_ref, n_ref, lr_ref, op_ref, wc_ref, o_ref, s_ref, s_scr, *, n_chunks, pairs):
    c = RWKV_CHUNK

    @pl.when(pl.program_id(1) == 0)
    def _():
        s_scr[...] = jnp.zeros_like(s_scr)

    def body(ci, carry):
        sl = pl.ds(pl.multiple_of(ci * c, c), c)
        outs = []
        for p in range(pairs):
            s = s_scr[p]
            sb = s.astype(BF16)
            o = _dg(lr_ref[ci, p], sb, NT_DIMS)
            outs.append(o[:c] + o[c:])
            s_scr[p] = s * wc_ref[ci, p] + _dot(sb, m_ref[ci, p]) + n_ref[ci, p]
        o_ref[sl, :] = jnp.concatenate(outs, axis=1) + op_ref[sl, :]
        return carry

    lax.fori_loop(0, n_chunks, body, 0)
    s_ref[...] = s_scr[...]


DecodePlan = collections.namedtuple("DecodePlan", "pps sps n_pages page lam0")


def _rwkv_scan(sh, pw, batch, seq_len, tt_chunk, tt_state, qm, k_self, v_self, cache_k, cache_v,
               page_table, aw, lam0):
    rows = sh.shape[0]
    rw = pw["w0"].shape[1]
    pairs = rw // LANES
    c = RWKV_CHUNK
    n_all = rows // c
    nb = tt_chunk // c
    gcols = rw // LANES
    n_steps = (rows // tt_chunk) * pairs
    dbatch, n_pages = page_table.shape
    n_phys, page = cache_k.shape[0], cache_k.shape[1]
    assert (dbatch * n_pages) % n_steps == 0, "decode pages must spread evenly over the chunk grid"
    pps = dbatch * n_pages // n_steps
    assert n_pages % pps == 0
    dec = DecodePlan(pps=pps, sps=n_pages // pps, n_pages=n_pages, page=page, lam0=lam0)
    ck = cache_k.reshape(n_phys, page * A_HEADS, LANES)
    cv = cache_v.reshape(n_phys, page * A_HEADS, LANES)
    nrow = 2 * A_HEADS

    tile = pl.BlockSpec((tt_chunk, LANES), lambda i, p, *_: (i, p))
    mat = pl.BlockSpec((nb, None, LANES, LANES), lambda i, p, *_: (i, p, 0, 0))
    win = lambda g: pl.BlockSpec((tt_chunk, LANES), lambda i, p, *_: (i, g * gcols + p))
    lora = pl.BlockSpec((tt_chunk, LANES), lambda i, p, *_: (i, 3 * gcols))
    trow = lambda i: jnp.maximum(i * (tt_chunk // 8) - 1, 0)
    twin = lambda g: pl.BlockSpec((8, LANES), lambda i, p, *_: (trow(i), g * gcols + p))
    tlora = pl.BlockSpec((8, LANES), lambda i, p, *_: (trow(i), 3 * gcols))
    mwin = lambda g: pl.BlockSpec((1, LANES), lambda i, p, *_: (0, g * gcols + p))
    mlora = pl.BlockSpec((1, LANES), lambda i, p, *_: (0, 3 * gcols))
    vec = pl.BlockSpec((1, LANES), lambda i, p, *_: (0, p))
    up = pl.BlockSpec((LANES, LANES), lambda i, p, *_: (0, p))
    per_seq = lambda r: pl.BlockSpec((None, r, LANES), lambda i, p, *_: ((i * pairs + p) // dec.sps, 0, 0))
    cvec = lambda n: pl.BlockSpec((1, n), lambda i, p, *_: (0, 0))
    hbm = pl.BlockSpec(memory_space=pl.ANY)
    grid_spec = pltpu.PrefetchScalarGridSpec(
        num_scalar_prefetch=1, grid=(rows // tt_chunk, pairs),
        in_specs=[win(0), win(1), win(2), lora, twin(0), twin(1), twin(2), tlora,
                  mwin(0), mwin(1), mwin(2), mlora, vec, up, vec, up, vec, vec, vec,
                  per_seq(nrow), pl.BlockSpec((nrow, 1), lambda i, p, *_: (0, 0)),
                  per_seq(A_HEADS), per_seq(A_HEADS),
                  cvec(A_QK), cvec(A_QK), cvec(A_QK), cvec(A_QK), cvec(A_V), hbm, hbm],
        out_specs=[mat, mat, pl.BlockSpec((nb, None, 2 * c, LANES), lambda i, p, *_: (i, p, 0, 0)), tile,
                   pl.BlockSpec((nb, None, 1, LANES), lambda i, p, *_: (i, p, 0, 0)), tile,
                   per_seq(A_HEADS)],
        scratch_shapes=[pltpu.VMEM((2, pps, page * A_HEADS, LANES), F32),
                        pltpu.VMEM((2, pps, page * A_HEADS, LANES), F32),
                        pltpu.SemaphoreType.DMA((2, 2, pps)),
                        pltpu.VMEM((nrow, 1), F32), pltpu.VMEM((nrow, 1), F32),
                        pltpu.VMEM((nrow, LANES), F32)])
    m_, n_, lr_, op_, wc_, bonus, o_dec = pl.pallas_call(
        functools.partial(_rwkv_chunk_kernel, n_chunks=nb, tiles_per_seq=seq_len // tt_chunk,
                          pairs=pairs, dec=dec),
        out_shape=[jax.ShapeDtypeStruct((n_all, pairs, LANES, LANES), BF16),
                   jax.ShapeDtypeStruct((n_all, pairs, LANES, LANES), F32),
                   jax.ShapeDtypeStruct((n_all, pairs, 2 * c, LANES), BF16),
                   jax.ShapeDtypeStruct((rows, rw), F32),
                   jax.ShapeDtypeStruct((n_all, pairs, 1, LANES), F32),
                   jax.ShapeDtypeStruct((rows, rw), F32),
                   jax.ShapeDtypeStruct((dbatch, A_HEADS, LANES), F32)],
        grid_spec=grid_spec,
        compiler_params=_params("arbitrary", "arbitrary"),
        name="rwkv_chunk_decode",
    )(page_table, sh, sh, sh, sh, sh, sh, sh, sh, pw["mu"], pw["mu"], pw["mu"], pw["mu"],
      pw["w0"], pw["w2p"], pw["a0"], pw["a2p"], pw["k_k"], pw["k_a"], pw["r_k"],
      qm, aw["slopes16"], k_self, v_self, aw["lq1"], aw["lk1"], aw["lq2"], aw["lk2"], aw["subln_g"],
      ck, cv)

    nt = seq_len // tt_state
    ns = tt_state // c
    smat = lambda rws: pl.BlockSpec((ns, pairs, rws, LANES), lambda b, i: (b * nt + i, 0, 0, 0))
    wide = pl.BlockSpec((tt_state, rw), lambda b, i: (b * nt + i, 0))
    o, s = pl.pallas_call(
        functools.partial(_rwkv_state_kernel, n_chunks=ns, pairs=pairs),
        out_shape=[jax.ShapeDtypeStruct((rows, rw), F32),
                   jax.ShapeDtypeStruct((batch, pairs, LANES, LANES), F32)],
        grid=(batch, nt),
        in_specs=[smat(LANES), smat(LANES), smat(2 * c), wide, smat(1)],
        out_specs=[wide, pl.BlockSpec((None, pairs, LANES, LANES), lambda b, i: (b, 0, 0, 0))],
        scratch_shapes=[pltpu.VMEM((pairs, LANES, LANES), F32)],
        compiler_params=_params("parallel", "arbitrary"),
        name="rwkv_state",
    )(m_, n_, lr_, op_, wc_)
    s = s.reshape(batch, pairs, 2, R_HEAD, 2, R_HEAD)
    s = jnp.stack([s[:, :, 0, :, 0, :], s[:, :, 1, :, 1, :]], axis=2)
    return o, bonus, s.reshape(batch, 2 * pairs, R_HEAD, R_HEAD), o_dec


def _rwkv_step_kernel(s_ref, r_ref, lw_ref, k_ref, al_ref, be_ref, vcol_ref, o_ref, so_ref):
    s = s_ref[...]
    sa = jnp.sum(s * al_ref[...], axis=-1, keepdims=True)
    s_new = s * jnp.exp(lw_ref[...]) + sa * be_ref[...] + vcol_ref[...] * k_ref[...]
    so_ref[...] = s_new
    o_ref[...] = jnp.sum(s_new * r_ref[...], axis=-1, keepdims=True)


def _rwkv_step(state, r, lw, k, v, al, be):
    b, h, n, _ = state.shape
    vec = lambda x: x.reshape(b, h, 1, n)
    vspec = pl.BlockSpec((None, h, 1, n), lambda i: (i, 0, 0, 0))
    cspec = pl.BlockSpec((None, h, n, 1), lambda i: (i, 0, 0, 0))
    sspec = pl.BlockSpec((None, h, n, n), lambda i: (i, 0, 0, 0))
    o, s_new = pl.pallas_call(
        _rwkv_step_kernel,
        out_shape=[jax.ShapeDtypeStruct((b, h, n, 1), F32), jax.ShapeDtypeStruct(state.shape, F32)],
        grid=(b,),
        in_specs=[sspec, vspec, vspec, vspec, vspec, vspec, cspec],
        out_specs=[cspec, sspec],
        compiler_params=_params("parallel"), name="rwkv_step",
    )(state, vec(r), vec(lw), vec(k), vec(al), vec(be), v.reshape(b, h, n, 1))
    return o.reshape(b, h * n), s_new


def _lambda(lq1, lk1, lq2, lk2, lam0):
    t1 = jnp.sum(lq1 * lk1, axis=-1, keepdims=True)
    t2 = jnp.sum(lq2 * lk2, axis=-1, keepdims=True)
    return jnp.exp(t1) - jnp.exp(t2) + lam0


def _diff_finish(o1, o2, lam, subln_g, lam0):
    o = o1 - lam * o2
    y = o * lax.rsqrt(jnp.mean(o * o, axis=-1, keepdims=True) + RMS_EPS)
    return y * subln_g * (1.0 - lam0)


ATTN_GROUP = 256
SUM_ROWS = 16
ATTN_KEYS = 1024
ATTN_AHEAD = 4
ALIBI_SPLIT = 256


def _attn_prompt_kernel(slopes_ref, q_ref, k_ref, v_ref, lq1, lk1, lq2, lk2, g_ref, o_ref,
                        kb, vt, *, tq, lam0):
    h = pl.program_id(1)
    qi = pl.program_id(2)
    seq_len = k_ref.shape[0]

    @pl.when(qi == 0)
    def _():
        slope = slopes_ref[h]
        pos = lax.broadcasted_iota(jnp.int32, (seq_len, LANES), 0)
        lane = lax.broadcasted_iota(jnp.int32, (seq_len, LANES), 1)
        lo = (pos % ALIBI_SPLIT).astype(F32) * slope
        hi = (pos - pos % ALIBI_SPLIT).astype(F32) * slope
        kb[:, :LANES] = k_ref[...].astype(BF16)
        n_alibi = 2 * len(LOG2E_TERMS)
        kb[:, LANES:] = jnp.where(lane >= n_alibi, 0.0, jnp.where(lane % 2 == 0, lo, hi)).astype(BF16)
        for c0 in range(0, seq_len, tq):
            vt[:LANES, c0:c0 + tq] = v_ref[c0:c0 + tq, :].T.astype(BF16)
        vt[LANES:, :] = jnp.ones((SUM_ROWS, seq_len), BF16)

    lane = lax.broadcasted_iota(jnp.int32, (tq, LANES), 1)
    q = q_ref[...]
    ones = jnp.zeros((tq, LANES), F32)
    for i, term in enumerate(LOG2E_TERMS):
        ones = jnp.where(lane // 2 == i, term, ones)
    ones = ones.astype(BF16)
    zero = jnp.zeros_like(q)
    q_aug = jnp.concatenate([
        jnp.concatenate([jnp.where(lane < A_QK, q, zero), ones], axis=1),
        jnp.concatenate([jnp.where(lane >= A_QK, q, zero), ones], axis=1)], axis=0)
    gw = ATTN_GROUP
    groups = range(2 * tq // gw)
    q_grp = [q_aug[g * gw:(g + 1) * gw, :] for g in groups]

    ksub = min(ATTN_KEYS, tq)

    def block(j, carry, diagonal):
        m, acc = (list(x) for x in carry)
        steps = []
        for a in range(tq // ksub):
            for g in groups:
                q0 = (g * gw) % tq
                rows = min(ksub, q0 + gw - a * ksub) if diagonal else ksub
                if rows > 0:
                    steps.append((a, g, rows, diagonal and (a * ksub + rows - 1 > q0)))

        def scores(a, g, rows, masked):
            ks = pl.ds(pl.multiple_of(j * tq + a * ksub, ksub), rows)
            s = _dg(kb[ks, :], q_grp[g], NT_DIMS)
            if masked:
                r_i = lax.broadcasted_iota(jnp.int32, (rows, gw), 0) + a * ksub
                c_i = lax.broadcasted_iota(jnp.int32, (rows, gw), 1) + (g * gw) % tq
                s = jnp.where(r_i <= c_i, s, NEG_INF)
            return s

        ahead = [scores(*st) for st in steps[:ATTN_AHEAD]]
        for idx, (a, g, rows, _) in enumerate(steps):
            if idx + ATTN_AHEAD < len(steps):
                ahead.append(scores(*steps[idx + ATTN_AHEAD]))
            s = ahead.pop(0)
            ks = pl.ds(pl.multiple_of(j * tq + a * ksub, ksub), rows)
            m_new = jnp.maximum(m[g], jnp.max(s, axis=0, keepdims=True))
            p = jnp.exp2(s - m_new)
            corr = jnp.exp2(m[g] - m_new)
            acc[g] = acc[g] * corr + _dot(vt[:, ks], p.astype(BF16))
            m[g] = m_new
        return tuple(m), tuple(acc)

    init = (tuple(jnp.full((1, gw), NEG_INF, F32) for _ in groups),
            tuple(jnp.zeros((LANES + SUM_ROWS, gw), F32) for _ in groups))
    carry = lax.fori_loop(0, qi, lambda j, cr: block(j, cr, False), init)
    _, acc = block(qi, carry, True)

    o = jnp.concatenate([a[:LANES] / a[LANES:LANES + 1] for a in acc], axis=1).T
    lam = _lambda(lq1[...], lk1[...], lq2[...], lk2[...], lam0)
    o_ref[...] = _diff_finish(o[:tq], o[tq:], lam, g_ref[...], lam0)


def _attn_prompt(q, k, v, aw, batch, seq_len, tq, lam0):
    rows, width = k.shape
    nq = seq_len // tq
    qspec = pl.BlockSpec((tq, LANES), lambda b, h, i, *_: (b * nq + i, h))
    kvspec = pl.BlockSpec((seq_len, LANES), lambda b, h, i, *_: (b, h))
    vec = lambda n: pl.BlockSpec((1, n), lambda b, h, i, *_: (0, 0))
    grid_spec = pltpu.PrefetchScalarGridSpec(
        num_scalar_prefetch=1, grid=(batch, A_HEADS, nq),
        in_specs=[qspec, kvspec, kvspec, vec(A_QK), vec(A_QK), vec(A_QK), vec(A_QK), vec(A_V)],
        out_specs=qspec,
        scratch_shapes=[pltpu.VMEM((seq_len, 2 * LANES), BF16), pltpu.VMEM((LANES + SUM_ROWS, seq_len), BF16)])
    return pl.pallas_call(
        functools.partial(_attn_prompt_kernel, tq=tq, lam0=lam0),
        out_shape=jax.ShapeDtypeStruct((rows, width), F32),
        grid_spec=grid_spec,
        compiler_params=_params("parallel", "parallel", "arbitrary"),
        name="attn_prompt",
    )(aw["slopes"], q, k, v, aw["lq1"], aw["lk1"], aw["lq2"], aw["lk2"], aw["subln_g"])


def _merge_kernel(or_ref, bo_ref, zr_ref, oa_ref, za_ref, ga_ref, gr_ref,
                  lng_ref, lnb_ref, wr_ref, wa_ref, o_ref):
    o_r = or_ref[...]
    inv_n = 1.0 / R_HEAD
    mean = _head_sums(o_r) * inv_n
    d = o_r - mean
    var = _head_sums(d * d) * inv_n
    o_r = d * lax.rsqrt(var + GN_EPS) * lng_ref[...] + lnb_ref[...] + bo_ref[...]
    x_r = (o_r * _silu(zr_ref[...])).astype(BF16)
    x_a = (oa_ref[...] * _silu(za_ref[...])).astype(BF16)
    y_r = _dot(x_r, wr_ref[...])
    y_a = _dot(x_a, wa_ref[...])
    o_ref[...] = (_sigmoid(ga_ref[...]) * y_a + _sigmoid(gr_ref[...]) * y_r).astype(o_ref.dtype)


def _merge(o_r, bonus, o_a, zr, zag, mw, tm):
    rows, rw = o_r.shape
    d = mw["w_r_up"].shape[1]
    tile = pl.BlockSpec((tm, rw), lambda i: (i, 0))
    gate = lambda col0: pl.BlockSpec((pl.Element(tm), pl.Element(d)),
                                     lambda i: (pl.multiple_of(i * tm, 8), col0))
    const = lambda shape: pl.BlockSpec(shape, lambda i: (0,) * len(shape))
    return pl.pallas_call(
        _merge_kernel,
        out_shape=jax.ShapeDtypeStruct((rows, d), BF16),
        grid=(rows // tm,),
        in_specs=[tile, tile, tile, tile, tile, gate(rw), gate(rw + d),
                  const((1, rw)), const((1, rw)),
                  const(mw["w_r_up"].shape), const(mw["w_a_up"].shape)],
        out_specs=pl.BlockSpec((tm, d), lambda i: (i, 0)),
        compiler_params=_params("parallel"), name="merge",
    )(o_r, bonus, zr, o_a, zag, zag, zag, mw["lnx_g"], mw["lnx_b"], mw["w_r_up"], mw["w_a_up"])


def _out_kernel(x_ref, m_ref, p_ref, wo_ref, wp_ref, wg_ref, g_ref, o_ref):
    h = x_ref[...] + _dot(m_ref[...], wo_ref[...])
    ple = _dot(p_ref[...].astype(BF16), wp_ref[...])
    h = h + ple * _sigmoid(_dot(h.astype(BF16), wg_ref[...]))
    y = h * lax.rsqrt(jnp.mean(h * h, axis=-1, keepdims=True) + RMS_EPS)
    o_ref[...] = y * g_ref[...]


def _out_stage(x, m, p, ow, tm):
    rows, d = x.shape
    pd = p.shape[1]
    const = lambda shape: pl.BlockSpec(shape, lambda i: (0,) * len(shape),
                                       pipeline_mode=pl.Buffered(1))
    return pl.pallas_call(
        _out_kernel,
        out_shape=jax.ShapeDtypeStruct((rows, d), F32),
        grid=(rows // tm,),
        in_specs=[pl.BlockSpec((tm, d), lambda i: (i, 0)), pl.BlockSpec((tm, d), lambda i: (i, 0)),
                  pl.BlockSpec((tm, pd), lambda i: (i, 0)),
                  const((d, d)), const((pd, d)), const((d, d)), const((1, d))],
        out_specs=pl.BlockSpec((tm, d), lambda i: (i, 0)),
        compiler_params=_params("parallel"), name="out_stage",
    )(x, m, p, ow["w_out"], ow["w_ple"], ow["w_ple_gate"], ow["norm_final_g"])


def _tile(n, pref):
    return pref if n % pref == 0 else n


def kernel(x_prompt, x_sample, cache_k, cache_v, state_wkv, state_shift, page_table, p_prompt,
           p_sample, norm_in_g, w_in, mu_shift, w0, w2, a0, a2, k_k, k_a, r_k, lnx_g, lnx_b,
           w_rwkv_up, lambda_q1, lambda_k1, lambda_q2, lambda_k2, subln_g, w_attn_up, w_out,
           w_ple, w_ple_gate, norm_final_g):
    depth = w_in.shape[0]
    assert depth == 1, "single-layer trunk"
    layer = 0
    lam0 = 0.8 - 0.6 * math.exp(-0.3 * layer)
    bp, tp, d = x_prompt.shape
    bs, ts, _ = x_sample.shape
    assert ts == 1
    rw = w0.shape[1]
    sw = mu_shift.shape[1]
    aqk = A_HEADS * 2 * A_QK
    aw_ = A_HEADS * A_V
    row = lambda x: x.reshape(1, -1)

    w_all = w_in[layer].astype(BF16)
    c_zr, c_q, c_k, c_v, c_za = sw, sw + rw, sw + rw + aqk, sw + rw + 2 * aqk, sw + rw + 2 * aqk + aw_
    n_zag = w_all.shape[1] - c_za

    lora = w2.shape[1]
    zpad = jnp.zeros((LANES - lora, rw), BF16)
    pw = dict(mu=row(mu_shift[layer]), w0=row(w0[layer]), a0=row(a0[layer]), k_k=row(k_k[layer]),
              k_a=row(k_a[layer]), r_k=row(r_k[layer]),
              w2p=jnp.concatenate([w2[layer].astype(BF16), zpad], axis=0),
              a2p=jnp.concatenate([zpad, a2[layer].astype(BF16)], axis=0))
    slopes = 2.0 ** (-8.0 * jnp.arange(1, A_HEADS + 1, dtype=F32) / A_HEADS)
    aw = dict(slopes=slopes, slopes16=jnp.tile(slopes, 2).reshape(2 * A_HEADS, 1),
              lq1=row(lambda_q1[layer]), lk1=row(lambda_k1[layer]), lq2=row(lambda_q2[layer]),
              lk2=row(lambda_k2[layer]), subln_g=row(subln_g[layer]))
    mw = dict(lnx_g=row(lnx_g[layer]), lnx_b=row(lnx_b[layer]),
              w_r_up=w_rwkv_up[layer].astype(BF16), w_a_up=w_attn_up[layer].astype(BF16))
    ow = dict(w_out=w_out[layer].astype(BF16), w_ple=w_ple[layer].astype(BF16),
              w_ple_gate=w_ple_gate[layer].astype(BF16), norm_final_g=row(norm_final_g))

    def project(x2d, tm):
        xn = _rms_cast(x2d, norm_in_g[layer], _tile(x2d.shape[0], 512))
        sh = _matmul(xn, w_all, 0, sw, F32, tm, 640, name="proj_shift")
        zr = _matmul(xn, w_all, c_zr, rw, F32, tm, 1024, name="proj_zr")
        q = _matmul(xn, w_all, c_q, aqk, BF16, tm, 1024, scale=A_QK ** -0.5 * LOG2E, name="proj_q")
        k = _matmul(xn, w_all, c_k, aqk, F32, tm, 1024, name="proj_k")
        v = _matmul(xn, w_all, c_v, aw_, F32, tm, 1024, name="proj_v")
        zag = _matmul(xn, w_all, c_za, n_zag, F32, tm, 1024, name="proj_gates")
        return sh, zr, q, k, v, zag

    xp = x_prompt.reshape(bp * tp, d)
    xs = x_sample.reshape(bs, d)
    sh, zr, q, k, v, zag = project(xp, _tile(bp * tp, 1024))
    sh_s, zr_s, q_s, k_s, v_s, zag_s = project(xs, bs)
    q_h = q_s.reshape(bs, 1, A_HEADS, 2, A_QK)
    sel = jnp.eye(2, dtype=q_s.dtype).reshape(1, 2, 1, 2, 1)
    qm = (q_h * sel).reshape(bs, 2 * A_HEADS, 2 * A_QK)

    o_r, bonus, wkv_p, o_dec = _rwkv_scan(
        sh, pw, bp, tp, _tile(tp, 512), _tile(tp, 512), qm, k_s.reshape(bs, A_HEADS, 2 * A_QK),
        v_s.reshape(bs, A_HEADS, A_V), cache_k[layer], cache_v[layer], page_table, aw, lam0)
    o_a = _attn_prompt(q, k, v, aw, bp, tp, _tile(tp, 1024), lam0)
    m = _merge(o_r, bonus, o_a, zr, zag, mw, _tile(bp * tp, 256))
    y_p = _out_stage(xp, m, p_prompt[layer].reshape(bp * tp, -1), ow, _tile(bp * tp, 256))
    y_prompt = y_p.reshape(bp, tp, d)
    k_prompt = k.reshape(1, bp, tp, A_HEADS, 2 * A_QK)
    v_prompt = v.reshape(1, bp, tp, A_HEADS, A_V)
    shift_prompt = sh.reshape(bp, tp, sw)[:, -1][None]

    r_, lw_, k_, v_, al_, be_, bonus_s = _rwkv_prep_sample(sh_s, state_shift[layer], pw)
    o_r, wkv_s = _rwkv_step(state_wkv[layer], r_, lw_, k_, v_, al_, be_)
    m = _merge(o_r, bonus_s, o_dec.reshape(bs, aw_), zr_s, zag_s, mw, bs)
    y_s = _out_stage(xs, m, p_sample[layer].reshape(bs, -1), ow, bs)
    y_sample = y_s.reshape(bs, ts, d)
    k_sample = k_s.reshape(1, bs, ts, A_HEADS, 2 * A_QK)
    v_sample = v_s.reshape(1, bs, ts, A_HEADS, A_V)

    return (y_prompt, y_sample, k_prompt, v_prompt, k_sample, v_sample,
            wkv_p[None], wkv_s[None], shift_prompt, sh_s[None])
```

```python
import collections
import functools
import math
import struct

import jax
import jax.numpy as jnp
from jax import lax
from jax.experimental import pallas as pl
from jax.experimental.pallas import tpu as pltpu

F32 = jnp.float32
BF16 = jnp.bfloat16

LANES = 128
A_HEADS = 8
A_QK = 64
A_V = 2 * A_QK
R_HEAD = 64
RMS_EPS = 1e-5
GN_EPS = 64e-5
NEG_INF = -1e30
RWKV_CHUNK = 64
VMEM_LIMIT = 56 * 1024 * 1024


def _f32(x):
    return struct.unpack("<f", struct.pack("<f", x))[0]


def _bf16_terms(x):
    terms, rest = [], _f32(x)
    for _ in range(3):
        bits = struct.unpack("<I", struct.pack("<f", rest))[0] & 0xFFFF0000
        terms.append(struct.unpack("<f", struct.pack("<I", bits))[0])
        rest = _f32(rest - terms[-1])
    assert rest == 0.0
    return terms


LOG2E = _f32(1.0 / math.log(2.0))
LOG2E_TERMS = _bf16_terms(LOG2E)

NT_DIMS = (((1,), (1,)), ((), ()))
TN_DIMS = (((0,), (0,)), ((), ()))


def _params(*sem):
    return pltpu.CompilerParams(dimension_semantics=sem, vmem_limit_bytes=VMEM_LIMIT)


def _dot(a, b):
    return jnp.dot(a, b, preferred_element_type=F32)


def _dg(a, b, dims):
    return lax.dot_general(a, b, dims, preferred_element_type=F32)


def _split_dot_rhs(a_bf16, x):
    hi = x.astype(BF16)
    lo = (x - hi.astype(F32)).astype(BF16)
    return _dot(a_bf16, hi) + _dot(a_bf16, lo)


def _sigmoid(x):
    return 1.0 / (1.0 + jnp.exp(-x))


def _silu(x):
    return x * _sigmoid(x)


def _head_sums(x):
    width = x.shape[1]
    win = 2 * LANES if width % (2 * LANES) == 0 else LANES
    i = lax.broadcasted_iota(jnp.int32, (win, win), 0)
    j = lax.broadcasted_iota(jnp.int32, (win, win), 1)
    gmat = ((i // R_HEAD) == (j // R_HEAD)).astype(BF16)
    hi = x.astype(BF16)
    lo = (x - hi.astype(F32)).astype(BF16)
    parts = []
    for c0 in range(0, width, win):
        h, l = hi[:, c0:c0 + win], lo[:, c0:c0 + win]
        if win == LANES:
            parts.append(_dot(jnp.concatenate([h, l], axis=1), jnp.concatenate([gmat, gmat], axis=0)))
        else:
            parts.append(_dot(h, gmat) + _dot(l, gmat))
    return jnp.concatenate(parts, axis=1)


def _rms_kernel(x_ref, g_ref, o_ref):
    x = x_ref[...]
    y = x * lax.rsqrt(jnp.mean(x * x, axis=-1, keepdims=True) + RMS_EPS)
    o_ref[...] = (y * g_ref[...]).astype(o_ref.dtype)


def _rms_cast(x, g, tm):
    m, d = x.shape
    return pl.pallas_call(
        _rms_kernel,
        out_shape=jax.ShapeDtypeStruct((m, d), BF16),
        grid=(m // tm,),
        in_specs=[pl.BlockSpec((tm, d), lambda i: (i, 0)),
                  pl.BlockSpec((1, d), lambda i: (0, 0))],
        out_specs=pl.BlockSpec((tm, d), lambda i: (i, 0)),
        compiler_params=_params("parallel"),
        name="rms_cast",
    )(x, g.reshape(1, d))


def _mm_kernel(a_ref, b_ref, o_ref, *, scale):
    acc = _dot(a_ref[...], b_ref[...])
    if scale != 1.0:
        acc = acc * scale
    o_ref[...] = acc.astype(o_ref.dtype)


def _matmul(a, b, col0, n, out_dtype, tm, tn, scale=1.0, name="matmul"):
    m, k = a.shape
    mode = dict(pipeline_mode=pl.Buffered(1)) if tn == n else {}
    return pl.pallas_call(
        functools.partial(_mm_kernel, scale=scale),
        out_shape=jax.ShapeDtypeStruct((m, n), out_dtype),
        grid=(m // tm, n // tn),
        in_specs=[pl.BlockSpec((tm, k), lambda i, j: (i, 0)),
                  pl.BlockSpec((pl.Element(k), pl.Element(tn)),
                               lambda i, j: (0, pl.multiple_of(col0 + j * tn, LANES)), **mode)],
        out_specs=pl.BlockSpec((tm, tn), lambda i, j: (i, j)),
        compiler_params=_params("parallel", "parallel"),
        name=name,
    )(a, b)


def _rwkv_token_math(sh3, lora, prev3, prev_lora, mu3, mu_lora, w0, w2p, a0, a2p, k_k, k_a, r_k):
    lerp = lambda x, p, mu: x + mu * (p - x)
    r, kr, vr = (lerp(x, p, mu) for x, p, mu in zip(sh3, prev3, mu3))
    lo = lerp(lora, prev_lora, mu_lora)
    z = w0 + _dot(jnp.tanh(lo).astype(BF16), w2p)
    softplus = jnp.maximum(-z, 0.0) + jnp.log1p(jnp.exp(-jnp.abs(z)))
    logw = -jnp.exp(-softplus - 0.5)
    a = _sigmoid(a0 + _dot(lo.astype(BF16), a2p))
    kk = kr * k_k
    norm = jnp.maximum(jnp.sqrt(_head_sums(kk * kk)), 1e-12)
    kk = kk / norm
    kmod = kr * (1.0 + (a - 1.0) * k_a)
    bonus = _head_sums(r * kmod * r_k) * vr
    return r, logw, kmod, vr, -kk, kk * a, bonus


def _rwkv_prep_sample_kernel(sh_ref, prev_ref, mu_ref, w0_ref, w2_ref, a0_ref, a2_ref,
                             kk_ref, ka_ref, rk_ref, r_o, lw_o, k_o, v_o, al_o, be_o, bo_o):
    rw = w0_ref.shape[1]
    win3 = lambda ref: tuple(ref[:, i * rw:(i + 1) * rw] for i in range(3))
    lo = lambda ref: ref[:, 3 * rw:]
    outs = _rwkv_token_math(win3(sh_ref), lo(sh_ref), win3(prev_ref), lo(prev_ref), win3(mu_ref),
                            lo(mu_ref), w0_ref[...], w2_ref[...], a0_ref[...], a2_ref[...],
                            kk_ref[...], ka_ref[...], rk_ref[...])
    for ref, val in zip((r_o, lw_o, k_o, v_o, al_o, be_o, bo_o), outs):
        ref[...] = val


def _rwkv_prep_sample(sh, prev, pw):
    rows, sw = sh.shape
    rw = pw["w0"].shape[1]
    const = lambda shape: pl.BlockSpec(shape, lambda *_: (0,) * len(shape))
    vec = const((1, rw))
    return pl.pallas_call(
        _rwkv_prep_sample_kernel, out_shape=[jax.ShapeDtypeStruct((rows, rw), F32)] * 7, grid=(1,),
        in_specs=[const((rows, sw)), const((rows, sw)), const((1, sw)), vec, const(pw["w2p"].shape), vec,
                  const(pw["a2p"].shape), vec, vec, vec],
        out_specs=[const((rows, rw))] * 7,
        compiler_params=_params("arbitrary"), name="rwkv_prep_sample",
    )(sh, prev, pw["mu"], pw["w0"], pw["w2p"], pw["a0"], pw["a2p"], pw["k_k"], pw["k_a"], pw["r_k"])


def _rwkv_chunk_kernel(pt_ref, r_sh, k_sh, v_sh, lo_sh, r_tl, k_tl, v_tl, lo_tl, mu_r, mu_k, mu_v, mu_lo,
                       w0_ref, w2_ref, a0_ref, a2_ref, kk_ref, ka_ref, rk_ref,
                       qm_ref, slope_ref, ks_ref, vs_ref, lq1, lk1, lq2, lk2, g_ref, ck_hbm, cv_hbm,
                       m_o, n_o, lr_o, op_o, wc_o, bo_o, od_o,
                       kbuf, vbuf, sem, m_scr, l_scr, acc_scr,
                       *, n_chunks, tiles_per_seq, pairs, dec):
    step = pl.program_id(0) * pairs + pl.program_id(1)
    slot = step % 2

    def page_copies(st, sl):
        seq = st // dec.sps
        first = (st % dec.sps) * dec.pps
        copies = []
        for i in range(dec.pps):
            phys = pt_ref[seq, first + i]
            copies.append(pltpu.make_async_copy(ck_hbm.at[phys], kbuf.at[sl, i], sem.at[sl, 0, i]))
            copies.append(pltpu.make_async_copy(cv_hbm.at[phys], vbuf.at[sl, i], sem.at[sl, 1, i]))
        return copies

    @pl.when(step == 0)
    def _():
        for cp in page_copies(step, slot):
            cp.start()

    @pl.when(step + 1 < pl.num_programs(0) * pairs)
    def _():
        for cp in page_copies(step + 1, 1 - slot):
            cp.start()

    @pl.when(step % dec.sps == 0)
    def _():
        m_scr[...] = jnp.full(m_scr.shape, NEG_INF, F32)
        l_scr[...] = jnp.zeros_like(l_scr)
        acc_scr[...] = jnp.zeros_like(acc_scr)

    for cp in page_copies(step, slot):
        cp.wait()

    nrow = 2 * A_HEADS
    pcols = dec.page * A_HEADS
    qm = qm_ref[...]
    dstate = {}

    def dec_scores():
        slope = slope_ref[...] * LOG2E
        r_i = lax.broadcasted_iota(jnp.int32, (nrow, pcols), 0)
        c_i = lax.broadcasted_iota(jnp.int32, (nrow, pcols), 1)
        valid = (c_i % A_HEADS) == (r_i % A_HEADS)
        tok = (c_i // A_HEADS).astype(F32)
        first = (step % dec.sps) * dec.pps
        scores = []
        for i in range(dec.pps):
            base = ((dec.n_pages - (first + i)) * dec.page).astype(F32)
            s = _dg(qm, kbuf[slot, i].astype(BF16), NT_DIMS) - slope * (base - tok)
            scores.append(jnp.where(valid, s, NEG_INF))
        dstate["scores"] = scores

    def dec_softmax():
        scores = dstate["scores"]
        m_old = m_scr[...]
        m_blk = functools.reduce(jnp.maximum, [jnp.max(s, axis=-1, keepdims=True) for s in scores])
        m_new = jnp.maximum(m_old, m_blk)
        corr = jnp.exp2(m_old - m_new)
        probs = [jnp.exp2(s - m_new) for s in scores]
        l_new = l_scr[...] * corr
        for pr in probs:
            l_new = l_new + jnp.sum(pr, axis=-1, keepdims=True)
        l_scr[...] = l_new
        m_scr[...] = m_new
        dstate["probs"] = [pr.astype(BF16) for pr in probs]
        dstate["corr"] = corr

    def dec_values():
        acc = acc_scr[...] * dstate["corr"]
        for i, pr in enumerate(dstate["probs"]):
            acc = acc + _dot(pr, vbuf[slot, i].astype(BF16))
        acc_scr[...] = acc

    seq_start = pl.program_id(0) % tiles_per_seq == 0
    trow = lax.broadcasted_iota(jnp.int32, r_sh.shape, 0)

    def prev_of(ref, tail_ref):
        first = jnp.where(seq_start, 0.0, tail_ref[7:8, :])
        return jnp.where(trow == 0, first, pltpu.roll(ref[...], 1, axis=0))

    dec_scores()
    r_all, lw_all, k_all, v_all, al_all, be_all, bonus = _rwkv_token_math(
        (r_sh[...], k_sh[...], v_sh[...]), lo_sh[...],
        (prev_of(r_sh, r_tl), prev_of(k_sh, k_tl), prev_of(v_sh, v_tl)), prev_of(lo_sh, lo_tl),
        (mu_r[...], mu_k[...], mu_v[...]), mu_lo[...], w0_ref[...], w2_ref[...], a0_ref[...],
        a2_ref[...], kk_ref[...], ka_ref[...], rk_ref[...])
    bo_o[...] = bonus

    c = RWKV_CHUNK
    two_c = 2 * c
    row = lax.broadcasted_iota(jnp.int32, (two_c, LANES), 0)
    col = lax.broadcasted_iota(jnp.int32, (two_c, LANES), 1)
    keep = (row >= c) == (col >= R_HEAD)
    strict = col < row
    incl = col <= row
    eye = (row == col).astype(F32)
    ti = lax.broadcasted_iota(jnp.int32, (c, c), 0)
    si = lax.broadcasted_iota(jnp.int32, (c, c), 1)
    ltri = (si <= ti).astype(BF16)

    def stack(x):
        return jnp.where(keep, jnp.concatenate([x, x], axis=0), 0.0).astype(BF16)

    cs = range(n_chunks)
    each = lambda fn, *lists: [fn(*xs) for xs in zip(*lists)]
    split = lambda x: [x[ci * c:(ci + 1) * c, :] for ci in cs]
    r, lw, k, v, al, be = (split(x) for x in (r_all, lw_all, k_all, v_all, al_all, be_all))
    cl = each(lambda x: _split_dot_rhs(ltri, x), lw)
    e_pos = each(jnp.exp, cl)
    e_neg = each(lambda x: jnp.exp(-x), cl)
    e_prev = each(lambda x, y: jnp.exp(x - y), cl, lw)
    wc = each(lambda x: x[c - 1:c, :], e_pos)
    kt = each(jnp.multiply, k, e_neg)
    bt = each(jnp.multiply, be, e_neg)
    la = each(lambda x, y: stack(x * y), al, e_prev)
    lr = each(lambda x, y: stack(x * y), r, e_pos)
    rb, rk, vs = each(stack, bt), each(stack, kt), each(stack, v)
    rbw = each(lambda x, y: stack(x * y), bt, wc)
    rkw = each(lambda x, y: stack(x * y), kt, wc)
    nt_dot = lambda x, y: _dg(x, y, NT_DIMS)
    a_ab = each(lambda x, y: jnp.where(strict, nt_dot(x, y), 0.0), la, rb)
    a_ak = each(lambda x, y: jnp.where(strict, nt_dot(x, y), 0.0).astype(BF16), la, rk)
    a_rb = each(lambda x, y: jnp.where(incl, nt_dot(x, y), 0.0).astype(BF16), lr, rb)
    a_rk = each(lambda x, y: jnp.where(incl, nt_dot(x, y), 0.0).astype(BF16), lr, rk)
    dec_softmax()
    p = each(lambda x: eye + x, a_ab)
    pw = a_ab
    for level in range(int(math.log2(c)) - 1):
        pwb = each(lambda x: x.astype(BF16), pw)
        pw = each(_dot, pwb, pwb)
        p = each(lambda x, y: x + _dot(x.astype(BF16), y.astype(BF16)), p, pw)
        if level == 1:
            dec_values()
    tb = each(lambda x: x.astype(BF16), p)
    ua = each(lambda x, y: _dot(x, y).astype(BF16), tb, la)
    akv = each(lambda x, y: _dot(x, y).astype(BF16), a_ak, vs)
    uv = each(lambda x, y: _dot(x, y).astype(BF16), tb, akv)
    lr2 = each(lambda x, y, z: (x.astype(F32) + _dot(y, z)).astype(BF16), lr, a_rb, ua)
    op = each(lambda x, y, z, w: _dot(x, y) + _dot(z, w), a_rb, uv, a_rk, vs)
    m2 = each(lambda x, y: _dg(x, y, TN_DIMS).astype(BF16), ua, rbw)
    n2 = each(lambda x, y, z, w: _dg(jnp.concatenate([x, y], axis=0),
                                     jnp.concatenate([z, w], axis=0), TN_DIMS), uv, vs, rbw, rkw)
    for ci in cs:
        lr_o[ci] = lr2[ci]
        op_o[pl.ds(ci * c, c), :] = op[ci][:c] + op[ci][c:]
        m_o[ci] = m2[ci]
        n_o[ci] = n2[ci]
        wc_o[ci] = wc[ci]

    @pl.when(step % dec.sps == dec.sps - 1)
    def _():
        k_self = jnp.concatenate([ks_ref[...], ks_ref[...]], axis=0).astype(BF16).astype(F32)
        v_self = jnp.concatenate([vs_ref[...], vs_ref[...]], axis=0).astype(BF16).astype(F32)
        s_self = jnp.sum(qm.astype(F32) * k_self, axis=-1, keepdims=True)
        m_run = m_scr[...]
        m_fin = jnp.maximum(m_run, s_self)
        c_fin = jnp.exp2(m_run - m_fin)
        p_self = jnp.exp2(s_self - m_fin)
        o = (acc_scr[...] * c_fin + p_self * v_self) / (l_scr[...] * c_fin + p_self)
        lam = _lambda(lq1[...], lk1[...], lq2[...], lk2[...], dec.lam0)
        od_o[...] = _diff_finish(o[:A_HEADS], o[A_HEADS:], lam, g_ref[...], dec.lam0)


def _rwkv_state_kernel(m_ref, n_ref, lr_ref, op_ref, wc_ref, o_ref, s_ref, s_scr, *, n_chunks, pairs):
    c = RWKV_CHUNK

    @pl.when(pl.program_id(1) == 0)
    def _():
        s_scr[...] = jnp.zeros_like(s_scr)

    def body(ci, carry):
        sl = pl.ds(pl.multiple_of(ci * c, c), c)
        outs = []
        for p in range(pairs):
            s = s_scr[p]
            sb = s.astype(BF16)
            o = _dg(lr_ref[ci, p], sb, NT_DIMS)
            outs.append(o[:c] + o[c:])
            s_scr[p] = s * wc_ref[ci, p] + _dot(sb, m_ref[ci, p]) + n_ref[ci, p]
        o_ref[sl, :] = jnp.concatenate(outs, axis=1) + op_ref[sl, :]
        return carry

    lax.fori_loop(0, n_chunks, body, 0)
    s_ref[...] = s_scr[...]


DecodePlan = collections.namedtuple("DecodePlan", "pps sps n_pages page lam0")


def _rwkv_scan(sh, pw, batch, seq_len, tt_chunk, tt_state, qm, k_self, v_self, cache_k, cache_v,
               page_table, aw, lam0):
    rows = sh.shape[0]
    rw = pw["w0"].shape[1]
    pairs = rw // LANES
    c = RWKV_CHUNK
    n_all = rows // c
    nb = tt_chunk // c
    gcols = rw // LANES
    n_steps = (rows // tt_chunk) * pairs
    dbatch, n_pages = page_table.shape
    n_phys, page = cache_k.shape[0], cache_k.shape[1]
    assert (dbatch * n_pages) % n_steps == 0, "decode pages must spread evenly over the chunk grid"
    pps = dbatch * n_pages // n_steps
    assert n_pages % pps == 0
    dec = DecodePlan(pps=pps, sps=n_pages // pps, n_pages=n_pages, page=page, lam0=lam0)
    ck = cache_k.reshape(n_phys, page * A_HEADS, LANES)
    cv = cache_v.reshape(n_phys, page * A_HEADS, LANES)
    nrow = 2 * A_HEADS

    tile = pl.BlockSpec((tt_chunk, LANES), lambda i, p, *_: (i, p))
    mat = pl.BlockSpec((nb, None, LANES, LANES), lambda i, p, *_: (i, p, 0, 0))
    win = lambda g: pl.BlockSpec((tt_chunk, LANES), lambda i, p, *_: (i, g * gcols + p))
    lora = pl.BlockSpec((tt_chunk, LANES), lambda i, p, *_: (i, 3 * gcols))
    trow = lambda i: jnp.maximum(i * (tt_chunk // 8) - 1, 0)
    twin = lambda g: pl.BlockSpec((8, LANES), lambda i, p, *_: (trow(i), g * gcols + p))
    tlora = pl.BlockSpec((8, LANES), lambda i, p, *_: (trow(i), 3 * gcols))
    mwin = lambda g: pl.BlockSpec((1, LANES), lambda i, p, *_: (0, g * gcols + p))
    mlora = pl.BlockSpec((1, LANES), lambda i, p, *_: (0, 3 * gcols))
    vec = pl.BlockSpec((1, LANES), lambda i, p, *_: (0, p))
    up = pl.BlockSpec((LANES, LANES), lambda i, p, *_: (0, p))
    per_seq = lambda r: pl.BlockSpec((None, r, LANES), lambda i, p, *_: ((i * pairs + p) // dec.sps, 0, 0))
    cvec = lambda n: pl.BlockSpec((1, n), lambda i, p, *_: (0, 0))
    hbm = pl.BlockSpec(memory_space=pl.ANY)
    grid_spec = pltpu.PrefetchScalarGridSpec(
        num_scalar_prefetch=1, grid=(rows // tt_chunk, pairs),
        in_specs=[win(0), win(1), win(2), lora, twin(0), twin(1), twin(2), tlora,
                  mwin(0), mwin(1), mwin(2), mlora, vec, up, vec, up, vec, vec, vec,
                  per_seq(nrow), pl.BlockSpec((nrow, 1), lambda i, p, *_: (0, 0)),
                  per_seq(A_HEADS), per_seq(A_HEADS),
                  cvec(A_QK), cvec(A_QK), cvec(A_QK), cvec(A_QK), cvec(A_V), hbm, hbm],
        out_specs=[mat, mat, pl.BlockSpec((nb, None, 2 * c, LANES), lambda i, p, *_: (i, p, 0, 0)), tile,
                   pl.BlockSpec((nb, None, 1, LANES), lambda i, p, *_: (i, p, 0, 0)), tile,
                   per_seq(A_HEADS)],
        scratch_shapes=[pltpu.VMEM((2, pps, page * A_HEADS, LANES), F32),
                        pltpu.VMEM((2, pps, page * A_HEADS, LANES), F32),
                        pltpu.SemaphoreType.DMA((2, 2, pps)),
                        pltpu.VMEM((nrow, 1), F32), pltpu.VMEM((nrow, 1), F32),
                        pltpu.VMEM((nrow, LANES), F32)])
    m_, n_, lr_, op_, wc_, bonus, o_dec = pl.pallas_call(
        functools.partial(_rwkv_chunk_kernel, n_chunks=nb, tiles_per_seq=seq_len // tt_chunk,
                          pairs=pairs, dec=dec),
        out_shape=[jax.ShapeDtypeStruct((n_all, pairs, LANES, LANES), BF16),
                   jax.ShapeDtypeStruct((n_all, pairs, LANES, LANES), F32),
                   jax.ShapeDtypeStruct((n_all, pairs, 2 * c, LANES), BF16),
                   jax.ShapeDtypeStruct((rows, rw), F32),
                   jax.ShapeDtypeStruct((n_all, pairs, 1, LANES), F32),
                   jax.ShapeDtypeStruct((rows, rw), F32),
                   jax.ShapeDtypeStruct((dbatch, A_HEADS, LANES), F32)],
        grid_spec=grid_spec,
        compiler_params=_params("arbitrary", "arbitrary"),
        name="rwkv_chunk_decode",
    )(page_table, sh, sh, sh, sh, sh, sh, sh, sh, pw["mu"], pw["mu"], pw["mu"], pw["mu"],
      pw["w0"], pw["w2p"], pw["a0"], pw["a2p"], pw["k_k"], pw["k_a"], pw["r_k"],
      qm, aw["slopes16"], k_self, v_self, aw["lq1"], aw["lk1"], aw["lq2"], aw["lk2"], aw["subln_g"],
      ck, cv)

    nt = seq_len // tt_state
    ns = tt_state // c
    smat = lambda rws: pl.BlockSpec((ns, pairs, rws, LANES), lambda b, i: (b * nt + i, 0, 0, 0))
    wide = pl.BlockSpec((tt_state, rw), lambda b, i: (b * nt + i, 0))
    o, s = pl.pallas_call(
        functools.partial(_rwkv_state_kernel, n_chunks=ns, pairs=pairs),
        out_shape=[jax.ShapeDtypeStruct((rows, rw), F32),
                   jax.ShapeDtypeStruct((batch, pairs, LANES, LANES), F32)],
        grid=(batch, nt),
        in_specs=[smat(LANES), smat(LANES), smat(2 * c), wide, smat(1)],
        out_specs=[wide, pl.BlockSpec((None, pairs, LANES, LANES), lambda b, i: (b, 0, 0, 0))],
        scratch_shapes=[pltpu.VMEM((pairs, LANES, LANES), F32)],
        compiler_params=_params("parallel", "arbitrary"),
        name="rwkv_state",
    )(m_, n_, lr_, op_, wc_)
    s = s.reshape(batch, pairs, 2, R_HEAD, 2, R_HEAD)
    s = jnp.stack([s[:, :, 0, :, 0, :], s[:, :, 1, :, 1, :]], axis=2)
    return o, bonus, s.reshape(batch, 2 * pairs, R_HEAD, R_HEAD), o_dec


def _rwkv_step_kernel(s_ref, r_ref, lw_ref, k_ref, al_ref, be_ref, vcol_ref, o_ref, so_ref):
    s = s_ref[...]
    sa = jnp.sum(s * al_ref[...], axis=-1, keepdims=True)
    s_new = s * jnp.exp(lw_ref[...]) + sa * be_ref[...] + vcol_ref[...] * k_ref[...]
    so_ref[...] = s_new
    o_ref[...] = jnp.sum(s_new * r_ref[...], axis=-1, keepdims=True)


def _rwkv_step(state, r, lw, k, v, al, be):
    b, h, n, _ = state.shape
    heads = b * h
    hb = _tile(heads, 64)
    vec = lambda x: x.reshape(heads, 1, n)
    vspec = pl.BlockSpec((hb, 1, n), lambda i: (i, 0, 0))
    cspec = pl.BlockSpec((hb, n, 1), lambda i: (i, 0, 0))
    sspec = pl.BlockSpec((hb, n, n), lambda i: (i, 0, 0))
    o, s_new = pl.pallas_call(
        _rwkv_step_kernel,
        out_shape=[jax.ShapeDtypeStruct((heads, n, 1), F32), jax.ShapeDtypeStruct((heads, n, n), F32)],
        grid=(heads // hb,),
        in_specs=[sspec, vspec, vspec, vspec, vspec, vspec, cspec],
        out_specs=[cspec, sspec],
        compiler_params=_params("parallel"), name="rwkv_step",
    )(state.reshape(heads, n, n), vec(r), vec(lw), vec(k), vec(al), vec(be), v.reshape(heads, n, 1))
    return o.reshape(b, h * n), s_new.reshape(state.shape)


def _lambda(lq1, lk1, lq2, lk2, lam0):
    t1 = jnp.sum(lq1 * lk1, axis=-1, keepdims=True)
    t2 = jnp.sum(lq2 * lk2, axis=-1, keepdims=True)
    return jnp.exp(t1) - jnp.exp(t2) + lam0


def _diff_finish(o1, o2, lam, subln_g, lam0):
    o = o1 - lam * o2
    y = o * lax.rsqrt(jnp.mean(o * o, axis=-1, keepdims=True) + RMS_EPS)
    return y * subln_g * (1.0 - lam0)


ATTN_GROUP = 256
SUM_ROWS = 16
ATTN_KEYS = 1024
ATTN_AHEAD = 4
ALIBI_SPLIT = 256


def _attn_prompt_kernel(slopes_ref, q_ref, k_ref, v_ref, lq1, lk1, lq2, lk2, g_ref, o_ref,
                        kb, vt, *, tq, lam0):
    h = pl.program_id(1)
    qi = pl.program_id(2)
    seq_len = k_ref.shape[0]

    @pl.when(qi == 0)
    def _():
        slope = slopes_ref[h]
        pos = lax.broadcasted_iota(jnp.int32, (seq_len, LANES), 0)
        lane = lax.broadcasted_iota(jnp.int32, (seq_len, LANES), 1)
        lo = (pos % ALIBI_SPLIT).astype(F32) * slope
        hi = (pos - pos % ALIBI_SPLIT).astype(F32) * slope
        kb[:, :LANES] = k_ref[...].astype(BF16)
        n_alibi = 2 * len(LOG2E_TERMS)
        kb[:, LANES:] = jnp.where(lane >= n_alibi, 0.0, jnp.where(lane % 2 == 0, lo, hi)).astype(BF16)
        for c0 in range(0, seq_len, tq):
            vt[:LANES, c0:c0 + tq] = v_ref[c0:c0 + tq, :].T.astype(BF16)
        vt[LANES:, :] = jnp.ones((SUM_ROWS, seq_len), BF16)

    lane = lax.broadcasted_iota(jnp.int32, (tq, LANES), 1)
    q = q_ref[...]
    ones = jnp.zeros((tq, LANES), F32)
    for i, term in enumerate(LOG2E_TERMS):
        ones = jnp.where(lane // 2 == i, term, ones)
    ones = ones.astype(BF16)
    zero = jnp.zeros_like(q)
    q_aug = jnp.concatenate([
        jnp.concatenate([jnp.where(lane < A_QK, q, zero), ones], axis=1),
        jnp.concatenate([jnp.where(lane >= A_QK, q, zero), ones], axis=1)], axis=0)
    gw = ATTN_GROUP
    groups = range(2 * tq // gw)
    q_grp = [q_aug[g * gw:(g + 1) * gw, :] for g in groups]

    ksub = min(ATTN_KEYS, tq)

    def block(j, carry, diagonal):
        m, acc = (list(x) for x in carry)
        steps = []
        for a in range(tq // ksub):
            for g in groups:
                q0 = (g * gw) % tq
                rows = min(ksub, q0 + gw - a * ksub) if diagonal else ksub
                if rows > 0:
                    steps.append((a, g, rows, diagonal and (a * ksub + rows - 1 > q0)))

        def scores(a, g, rows, masked):
            ks = pl.ds(pl.multiple_of(j * tq + a * ksub, ksub), rows)
            s = _dg(kb[ks, :], q_grp[g], NT_DIMS)
            if masked:
                r_i = lax.broadcasted_iota(jnp.int32, (rows, gw), 0) + a * ksub
                c_i = lax.broadcasted_iota(jnp.int32, (rows, gw), 1) + (g * gw) % tq
                s = jnp.where(r_i <= c_i, s, NEG_INF)
            return s

        ahead = [scores(*st) for st in steps[:ATTN_AHEAD]]
        for idx, (a, g, rows, _) in enumerate(steps):
            if idx + ATTN_AHEAD < len(steps):
                ahead.append(scores(*steps[idx + ATTN_AHEAD]))
            s = ahead.pop(0)
            ks = pl.ds(pl.multiple_of(j * tq + a * ksub, ksub), rows)
            m_new = jnp.maximum(m[g], jnp.max(s, axis=0, keepdims=True))
            p = jnp.exp2(s - m_new)
            corr = jnp.exp2(m[g] - m_new)
            acc[g] = acc[g] * corr + _dot(vt[:, ks], p.astype(BF16))
            m[g] = m_new
        return tuple(m), tuple(acc)

    init = (tuple(jnp.full((1, gw), NEG_INF, F32) for _ in groups),
            tuple(jnp.zeros((LANES + SUM_ROWS, gw), F32) for _ in groups))
    carry = lax.fori_loop(0, qi, lambda j, cr: block(j, cr, False), init)
    _, acc = block(qi, carry, True)

    o = jnp.concatenate([a[:LANES] / a[LANES:LANES + 1] for a in acc], axis=1).T
    lam = _lambda(lq1[...], lk1[...], lq2[...], lk2[...], lam0)
    o_ref[...] = _diff_finish(o[:tq], o[tq:], lam, g_ref[...], lam0)


def _attn_prompt(q, k, v, aw, batch, seq_len, tq, lam0):
    rows, width = k.shape
    nq = seq_len // tq
    qspec = pl.BlockSpec((tq, LANES), lambda b, h, i, *_: (b * nq + i, h))
    kvspec = pl.BlockSpec((seq_len, LANES), lambda b, h, i, *_: (b, h))
    vec = lambda n: pl.BlockSpec((1, n), lambda b, h, i, *_: (0, 0))
    grid_spec = pltpu.PrefetchScalarGridSpec(
        num_scalar_prefetch=1, grid=(batch, A_HEADS, nq),
        in_specs=[qspec, kvspec, kvspec, vec(A_QK), vec(A_QK), vec(A_QK), vec(A_QK), vec(A_V)],
        out_specs=qspec,
        scratch_shapes=[pltpu.VMEM((seq_len, 2 * LANES), BF16), pltpu.VMEM((LANES + SUM_ROWS, seq_len), BF16)])
    return pl.pallas_call(
        functools.partial(_attn_prompt_kernel, tq=tq, lam0=lam0),
        out_shape=jax.ShapeDtypeStruct((rows, width), F32),
        grid_spec=grid_spec,
        compiler_params=_params("parallel", "parallel", "arbitrary"),
        name="attn_prompt",
    )(aw["slopes"], q, k, v, aw["lq1"], aw["lk1"], aw["lq2"], aw["lk2"], aw["subln_g"])


def _merge_kernel(or_ref, bo_ref, zr_ref, oa_ref, za_ref, ga_ref, gr_ref,
                  lng_ref, lnb_ref, wr_ref, wa_ref, o_ref):
    o_r = or_ref[...]
    inv_n = 1.0 / R_HEAD
    mean = _head_sums(o_r) * inv_n
    d = o_r - mean
    var = _head_sums(d * d) * inv_n
    o_r = d * lax.rsqrt(var + GN_EPS) * lng_ref[...] + lnb_ref[...] + bo_ref[...]
    x_r = (o_r * _silu(zr_ref[...])).astype(BF16)
    x_a = (oa_ref[...] * _silu(za_ref[...])).astype(BF16)
    y_r = _dot(x_r, wr_ref[...])
    y_a = _dot(x_a, wa_ref[...])
    o_ref[...] = (_sigmoid(ga_ref[...]) * y_a + _sigmoid(gr_ref[...]) * y_r).astype(o_ref.dtype)


def _merge(o_r, bonus, o_a, zr, zag, mw, tm):
    rows, rw = o_r.shape
    d = mw["w_r_up"].shape[1]
    tile = pl.BlockSpec((tm, rw), lambda i: (i, 0))
    gate = lambda col0: pl.BlockSpec((pl.Element(tm), pl.Element(d)),
                                     lambda i: (pl.multiple_of(i * tm, 8), col0))
    const = lambda shape: pl.BlockSpec(shape, lambda i: (0,) * len(shape))
    return pl.pallas_call(
        _merge_kernel,
        out_shape=jax.ShapeDtypeStruct((rows, d), BF16),
        grid=(rows // tm,),
        in_specs=[tile, tile, tile, tile, tile, gate(rw), gate(rw + d),
                  const((1, rw)), const((1, rw)),
                  const(mw["w_r_up"].shape), const(mw["w_a_up"].shape)],
        out_specs=pl.BlockSpec((tm, d), lambda i: (i, 0)),
        compiler_params=_params("parallel"), name="merge",
    )(o_r, bonus, zr, o_a, zag, zag, zag, mw["lnx_g"], mw["lnx_b"], mw["w_r_up"], mw["w_a_up"])


def _out_kernel(x_ref, m_ref, p_ref, wo_ref, wp_ref, wg_ref, g_ref, o_ref):
    h = x_ref[...] + _dot(m_ref[...], wo_ref[...])
    ple = _dot(p_ref[...].astype(BF16), wp_ref[...])
    h = h + ple * _sigmoid(_dot(h.astype(BF16), wg_ref[...]))
    y = h * lax.rsqrt(jnp.mean(h * h, axis=-1, keepdims=True) + RMS_EPS)
    o_ref[...] = y * g_ref[...]


def _out_stage(x, m, p, ow, tm):
    rows, d = x.shape
    pd = p.shape[1]
    const = lambda shape: pl.BlockSpec(shape, lambda i: (0,) * len(shape),
                                       pipeline_mode=pl.Buffered(1))
    return pl.pallas_call(
        _out_kernel,
        out_shape=jax.ShapeDtypeStruct((rows, d), F32),
        grid=(rows // tm,),
        in_specs=[pl.BlockSpec((tm, d), lambda i: (i, 0)), pl.BlockSpec((tm, d), lambda i: (i, 0)),
                  pl.BlockSpec((tm, pd), lambda i: (i, 0)),
                  const((d, d)), const((pd, d)), const((d, d)), const((1, d))],
        out_specs=pl.BlockSpec((tm, d), lambda i: (i, 0)),
        compiler_params=_params("parallel"), name="out_stage",
    )(x, m, p, ow["w_out"], ow["w_ple"], ow["w_ple_gate"], ow["norm_final_g"])


def _tile(n, pref):
    return pref if n % pref == 0 else n


def kernel(x_prompt, x_sample, cache_k, cache_v, state_wkv, state_shift, page_table, p_prompt,
           p_sample, norm_in_g, w_in, mu_shift, w0, w2, a0, a2, k_k, k_a, r_k, lnx_g, lnx_b,
           w_rwkv_up, lambda_q1, lambda_k1, lambda_q2, lambda_k2, subln_g, w_attn_up, w_out,
           w_ple, w_ple_gate, norm_final_g):
    depth = w_in.shape[0]
    assert depth == 1, "single-layer trunk"
    layer = 0
    lam0 = 0.8 - 0.6 * math.exp(-0.3 * layer)
    bp, tp, d = x_prompt.shape
    bs, ts, _ = x_sample.shape
    assert ts == 1
    rw = w0.shape[1]
    sw = mu_shift.shape[1]
    aqk = A_HEADS * 2 * A_QK
    aw_ = A_HEADS * A_V
    row = lambda x: x.reshape(1, -1)

    w_all = w_in[layer].astype(BF16)
    c_zr, c_q, c_k, c_v, c_za = sw, sw + rw, sw + rw + aqk, sw + rw + 2 * aqk, sw + rw + 2 * aqk + aw_
    n_zag = w_all.shape[1] - c_za

    lora = w2.shape[1]
    zpad = jnp.zeros((LANES - lora, rw), BF16)
    pw = dict(mu=row(mu_shift[layer]), w0=row(w0[layer]), a0=row(a0[layer]), k_k=row(k_k[layer]),
              k_a=row(k_a[layer]), r_k=row(r_k[layer]),
              w2p=jnp.concatenate([w2[layer].astype(BF16), zpad], axis=0),
              a2p=jnp.concatenate([zpad, a2[layer].astype(BF16)], axis=0))
    slopes = 2.0 ** (-8.0 * jnp.arange(1, A_HEADS + 1, dtype=F32) / A_HEADS)
    aw = dict(slopes=slopes, slopes16=jnp.tile(slopes, 2).reshape(2 * A_HEADS, 1),
              lq1=row(lambda_q1[layer]), lk1=row(lambda_k1[layer]), lq2=row(lambda_q2[layer]),
              lk2=row(lambda_k2[layer]), subln_g=row(subln_g[layer]))
    mw = dict(lnx_g=row(lnx_g[layer]), lnx_b=row(lnx_b[layer]),
              w_r_up=w_rwkv_up[layer].astype(BF16), w_a_up=w_attn_up[layer].astype(BF16))
    ow = dict(w_out=w_out[layer].astype(BF16), w_ple=w_ple[layer].astype(BF16),
              w_ple_gate=w_ple_gate[layer].astype(BF16), norm_final_g=row(norm_final_g))

    def project(x2d, tm):
        xn = _rms_cast(x2d, norm_in_g[layer], _tile(x2d.shape[0], 512))
        sh = _matmul(xn, w_all, 0, sw, F32, min(tm, 512), sw, name="proj_shift")
        zr = _matmul(xn, w_all, c_zr, rw, F32, tm, 1024, name="proj_zr")
        q = _matmul(xn, w_all, c_q, aqk, BF16, tm, 1024, scale=A_QK ** -0.5 * LOG2E, name="proj_q")
        k = _matmul(xn, w_all, c_k, aqk, F32, tm, 1024, name="proj_k")
        v = _matmul(xn, w_all, c_v, aw_, F32, tm, 1024, name="proj_v")
        zag = _matmul(xn, w_all, c_za, n_zag, F32, tm, 1024, name="proj_gates")
        return sh, zr, q, k, v, zag

    xp = x_prompt.reshape(bp * tp, d)
    xs = x_sample.reshape(bs, d)
    sh, zr, q, k, v, zag = project(xp, _tile(bp * tp, 1024))
    sh_s, zr_s, q_s, k_s, v_s, zag_s = project(xs, bs)
    q_h = q_s.reshape(bs, 1, A_HEADS, 2, A_QK)
    sel = jnp.eye(2, dtype=q_s.dtype).reshape(1, 2, 1, 2, 1)
    qm = (q_h * sel).reshape(bs, 2 * A_HEADS, 2 * A_QK)

    o_r, bonus, wkv_p, o_dec = _rwkv_scan(
        sh, pw, bp, tp, _tile(tp, 512), _tile(tp, 512), qm, k_s.reshape(bs, A_HEADS, 2 * A_QK),
        v_s.reshape(bs, A_HEADS, A_V), cache_k[layer], cache_v[layer], page_table, aw, lam0)
    o_a = _attn_prompt(q, k, v, aw, bp, tp, _tile(tp, 1024), lam0)
    m = _merge(o_r, bonus, o_a, zr, zag, mw, _tile(bp * tp, 256))
    y_p = _out_stage(xp, m, p_prompt[layer].reshape(bp * tp, -1), ow, _tile(bp * tp, 512))
    y_prompt = y_p.reshape(bp, tp, d)
    k_prompt = k.reshape(1, bp, tp, A_HEADS, 2 * A_QK)
    v_prompt = v.reshape(1, bp, tp, A_HEADS, A_V)
    shift_prompt = sh.reshape(bp, tp, sw)[:, -1][None]

    r_, lw_, k_, v_, al_, be_, bonus_s = _rwkv_prep_sample(sh_s, state_shift[layer], pw)
    o_r, wkv_s = _rwkv_step(state_wkv[layer], r_, lw_, k_, v_, al_, be_)
    m = _merge(o_r, bonus_s, o_dec.reshape(bs, aw_), zr_s, zag_s, mw, bs)
    y_s = _out_stage(xs, m, p_sample[layer].reshape(bs, -1), ow, bs)
    y_sample = y_s.reshape(bs, ts, d)
    k_sample = k_s.reshape(1, bs, ts, A_HEADS, 2 * A_QK)
    v_sample = v_s.reshape(1, bs, ts, A_HEADS, A_V)

    return (y_prompt, y_sample, k_prompt, v_prompt, k_sample, v_sample,
            wkv_p[None], wkv_s[None], shift_prompt, sh_s[None])
```

```python
import collections
import functools
import math
import struct

import jax
import jax.numpy as jnp
from jax import lax
from jax.experimental import pallas as pl
from jax.experimental.pallas import tpu as pltpu

F32 = jnp.float32
BF16 = jnp.bfloat16

LANES = 128
A_HEADS = 8
A_QK = 64
A_V = 2 * A_QK
R_HEAD = 64
RMS_EPS = 1e-5
GN_EPS = 64e-5
NEG_INF = -1e30
RWKV_CHUNK = 64
VMEM_LIMIT = 56 * 1024 * 1024


def _f32(x):
    return struct.unpack("<f", struct.pack("<f", x))[0]


def _bf16_terms(x):
    terms, rest = [], _f32(x)
    for _ in range(3):
        bits = struct.unpack("<I", struct.pack("<f", rest))[0] & 0xFFFF0000
        terms.append(struct.unpack("<f", struct.pack("<I", bits))[0])
        rest = _f32(rest - terms[-1])
    assert rest == 0.0
    return terms


LOG2E = _f32(1.0 / math.log(2.0))
LOG2E_TERMS = _bf16_terms(LOG2E)

NT_DIMS = (((1,), (1,)), ((), ()))
TN_DIMS = (((0,), (0,)), ((), ()))


def _params(*sem):
    return pltpu.CompilerParams(dimension_semantics=sem, vmem_limit_bytes=VMEM_LIMIT)


def _dot(a, b):
    return jnp.dot(a, b, preferred_element_type=F32)


def _dg(a, b, dims):
    return lax.dot_general(a, b, dims, preferred_element_type=F32)


def _split_dot_rhs(a_bf16, x):
    hi = x.astype(BF16)
    lo = (x - hi.astype(F32)).astype(BF16)
    return _dot(a_bf16, hi) + _dot(a_bf16, lo)


def _sigmoid(x):
    return 1.0 / (1.0 + jnp.exp(-x))


def _silu(x):
    return x * _sigmoid(x)


def _head_sums(x):
    width = x.shape[1]
    win = 2 * LANES if width % (2 * LANES) == 0 else LANES
    i = lax.broadcasted_iota(jnp.int32, (win, win), 0)
    j = lax.broadcasted_iota(jnp.int32, (win, win), 1)
    gmat = ((i // R_HEAD) == (j // R_HEAD)).astype(BF16)
    hi = x.astype(BF16)
    lo = (x - hi.astype(F32)).astype(BF16)
    parts = []
    for c0 in range(0, width, win):
        h, l = hi[:, c0:c0 + win], lo[:, c0:c0 + win]
        if win == LANES:
            parts.append(_dot(jnp.concatenate([h, l], axis=1), jnp.concatenate([gmat, gmat], axis=0)))
        else:
            parts.append(_dot(h, gmat) + _dot(l, gmat))
    return jnp.concatenate(parts, axis=1)


def _rms_proj_kernel(x_ref, g_ref, w_ref, xn_ref, o_ref):
    x = x_ref[...]
    y = x * lax.rsqrt(jnp.mean(x * x, axis=-1, keepdims=True) + RMS_EPS)
    xn = (y * g_ref[...]).astype(BF16)
    xn_ref[...] = xn
    o_ref[...] = _dot(xn, w_ref[...])


def _rms_proj(x, g, w, n, tm):
    m, d = x.shape
    return pl.pallas_call(
        _rms_proj_kernel,
        out_shape=[jax.ShapeDtypeStruct((m, d), BF16), jax.ShapeDtypeStruct((m, n), F32)],
        grid=(m // tm,),
        in_specs=[pl.BlockSpec((tm, d), lambda i: (i, 0)),
                  pl.BlockSpec((1, d), lambda i: (0, 0)),
                  pl.BlockSpec((pl.Element(d), pl.Element(n)), lambda i: (0, 0),
                               pipeline_mode=pl.Buffered(1))],
        out_specs=[pl.BlockSpec((tm, d), lambda i: (i, 0)), pl.BlockSpec((tm, n), lambda i: (i, 0))],
        compiler_params=_params("parallel"),
        name="rms_proj_shift",
    )(x, g.reshape(1, d), w)


def _mm_kernel(a_ref, b_ref, o_ref, *, scale):
    acc = _dot(a_ref[...], b_ref[...])
    if scale != 1.0:
        acc = acc * scale
    o_ref[...] = acc.astype(o_ref.dtype)


def _matmul(a, b, col0, n, out_dtype, tm, tn, scale=1.0, name="matmul"):
    m, k = a.shape
    mode = dict(pipeline_mode=pl.Buffered(1)) if tn == n else {}
    return pl.pallas_call(
        functools.partial(_mm_kernel, scale=scale),
        out_shape=jax.ShapeDtypeStruct((m, n), out_dtype),
        grid=(m // tm, n // tn),
        in_specs=[pl.BlockSpec((tm, k), lambda i, j: (i, 0)),
                  pl.BlockSpec((pl.Element(k), pl.Element(tn)),
                               lambda i, j: (0, pl.multiple_of(col0 + j * tn, LANES)), **mode)],
        out_specs=pl.BlockSpec((tm, tn), lambda i, j: (i, j)),
        compiler_params=_params("parallel", "parallel"),
        name=name,
    )(a, b)


def _rwkv_token_math(sh3, lora, prev3, prev_lora, mu3, mu_lora, w0, w2p, a0, a2p, k_k, k_a, r_k):
    lerp = lambda x, p, mu: x + mu * (p - x)
    r, kr, vr = (lerp(x, p, mu) for x, p, mu in zip(sh3, prev3, mu3))
    lo = lerp(lora, prev_lora, mu_lora)
    z = w0 + _dot(jnp.tanh(lo).astype(BF16), w2p)
    softplus = jnp.maximum(-z, 0.0) + jnp.log1p(jnp.exp(-jnp.abs(z)))
    logw = -jnp.exp(-softplus - 0.5)
    a = _sigmoid(a0 + _dot(lo.astype(BF16), a2p))
    kk = kr * k_k
    norm = jnp.maximum(jnp.sqrt(_head_sums(kk * kk)), 1e-12)
    kk = kk / norm
    kmod = kr * (1.0 + (a - 1.0) * k_a)
    bonus = _head_sums(r * kmod * r_k) * vr
    return r, logw, kmod, vr, -kk, kk * a, bonus


def _rwkv_prep_sample_kernel(sh_ref, prev_ref, mu_ref, w0_ref, w2_ref, a0_ref, a2_ref,
                             kk_ref, ka_ref, rk_ref, r_o, lw_o, k_o, v_o, al_o, be_o, bo_o):
    rw = w0_ref.shape[1]
    win3 = lambda ref: tuple(ref[:, i * rw:(i + 1) * rw] for i in range(3))
    lo = lambda ref: ref[:, 3 * rw:]
    outs = _rwkv_token_math(win3(sh_ref), lo(sh_ref), win3(prev_ref), lo(prev_ref), win3(mu_ref),
                            lo(mu_ref), w0_ref[...], w2_ref[...], a0_ref[...], a2_ref[...],
                            kk_ref[...], ka_ref[...], rk_ref[...])
    for ref, val in zip((r_o, lw_o, k_o, v_o, al_o, be_o, bo_o), outs):
        ref[...] = val


def _rwkv_prep_sample(sh, prev, pw):
    rows, sw = sh.shape
    rw = pw["w0"].shape[1]
    const = lambda shape: pl.BlockSpec(shape, lambda *_: (0,) * len(shape))
    vec = const((1, rw))
    return pl.pallas_call(
        _rwkv_prep_sample_kernel, out_shape=[jax.ShapeDtypeStruct((rows, rw), F32)] * 7, grid=(1,),
        in_specs=[const((rows, sw)), const((rows, sw)), const((1, sw)), vec, const(pw["w2p"].shape), vec,
                  const(pw["a2p"].shape), vec, vec, vec],
        out_specs=[const((rows, rw))] * 7,
        compiler_params=_params("arbitrary"), name="rwkv_prep_sample",
    )(sh, prev, pw["mu"], pw["w0"], pw["w2p"], pw["a0"], pw["a2p"], pw["k_k"], pw["k_a"], pw["r_k"])


def _rwkv_chunk_kernel(pt_ref, r_sh, k_sh, v_sh, lo_sh, r_tl, k_tl, v_tl, lo_tl, mu_r, mu_k, mu_v, mu_lo,
                       w0_ref, w2_ref, a0_ref, a2_ref, kk_ref, ka_ref, rk_ref,
                       qm_ref, slope_ref, ks_ref, vs_ref, lq1, lk1, lq2, lk2, g_ref, ck_hbm, cv_hbm,
                       m_o, n_o, lr_o, op_o, wc_o, bo_o, od_o,
                       kbuf, vbuf, sem, m_scr, l_scr, acc_scr,
                       *, n_chunks, tiles_per_seq, pairs, dec):
    step = pl.program_id(0) * pairs + pl.program_id(1)
    slot = step % 2

    def page_copies(st, sl):
        seq = st // dec.sps
        first = (st % dec.sps) * dec.pps
        copies = []
        for i in range(dec.pps):
            phys = pt_ref[seq, first + i]
            copies.append(pltpu.make_async_copy(ck_hbm.at[phys], kbuf.at[sl, i], sem.at[sl, 0, i]))
            copies.append(pltpu.make_async_copy(cv_hbm.at[phys], vbuf.at[sl, i], sem.at[sl, 1, i]))
        return copies

    @pl.when(step == 0)
    def _():
        for cp in page_copies(step, slot):
            cp.start()

    @pl.when(step + 1 < pl.num_programs(0) * pairs)
    def _():
        for cp in page_copies(step + 1, 1 - slot):
            cp.start()

    @pl.when(step % dec.sps == 0)
    def _():
        m_scr[...] = jnp.full(m_scr.shape, NEG_INF, F32)
        l_scr[...] = jnp.zeros_like(l_scr)
        acc_scr[...] = jnp.zeros_like(acc_scr)

    for cp in page_copies(step, slot):
        cp.wait()

    nrow = 2 * A_HEADS
    pcols = dec.page * A_HEADS
    qm = qm_ref[...]
    dstate = {}

    def dec_scores():
        slope = slope_ref[...] * LOG2E
        r_i = lax.broadcasted_iota(jnp.int32, (nrow, pcols), 0)
        c_i = lax.broadcasted_iota(jnp.int32, (nrow, pcols), 1)
        valid = (c_i % A_HEADS) == (r_i % A_HEADS)
        tok = (c_i // A_HEADS).astype(F32)
        first = (step % dec.sps) * dec.pps
        scores = []
        for i in range(dec.pps):
            base = ((dec.n_pages - (first + i)) * dec.page).astype(F32)
            s = _dg(qm, kbuf[slot, i].astype(BF16), NT_DIMS) - slope * (base - tok)
            scores.append(jnp.where(valid, s, NEG_INF))
        dstate["scores"] = scores

    def dec_softmax():
        scores = dstate["scores"]
        m_old = m_scr[...]
        m_blk = functools.reduce(jnp.maximum, [jnp.max(s, axis=-1, keepdims=True) for s in scores])
        m_new = jnp.maximum(m_old, m_blk)
        corr = jnp.exp2(m_old - m_new)
        probs = [jnp.exp2(s - m_new) for s in scores]
        l_new = l_scr[...] * corr
        for pr in probs:
            l_new = l_new + jnp.sum(pr, axis=-1, keepdims=True)
        l_scr[...] = l_new
        m_scr[...] = m_new
        dstate["probs"] = [pr.astype(BF16) for pr in probs]
        dstate["corr"] = corr

    def dec_values():
        acc = acc_scr[...] * dstate["corr"]
        for i, pr in enumerate(dstate["probs"]):
            acc = acc + _dot(pr, vbuf[slot, i].astype(BF16))
        acc_scr[...] = acc

    seq_start = pl.program_id(0) % tiles_per_seq == 0
    trow = lax.broadcasted_iota(jnp.int32, r_sh.shape, 0)

    def prev_of(ref, tail_ref):
        first = jnp.where(seq_start, 0.0, tail_ref[7:8, :])
        return jnp.where(trow == 0, first, pltpu.roll(ref[...], 1, axis=0))

    dec_scores()
    r_all, lw_all, k_all, v_all, al_all, be_all, bonus = _rwkv_token_math(
        (r_sh[...], k_sh[...], v_sh[...]), lo_sh[...],
        (prev_of(r_sh, r_tl), prev_of(k_sh, k_tl), prev_of(v_sh, v_tl)), prev_of(lo_sh, lo_tl),
        (mu_r[...], mu_k[...], mu_v[...]), mu_lo[...], w0_ref[...], w2_ref[...], a0_ref[...],
        a2_ref[...], kk_ref[...], ka_ref[...], rk_ref[...])
    bo_o[...] = bonus

    c = RWKV_CHUNK
    two_c = 2 * c
    row = lax.broadcasted_iota(jnp.int32, (two_c, LANES), 0)
    col = lax.broadcasted_iota(jnp.int32, (two_c, LANES), 1)
    keep = (row >= c) == (col >= R_HEAD)
    strict = col < row
    incl = col <= row
    eye = (row == col).astype(F32)
    ti = lax.broadcasted_iota(jnp.int32, (c, c), 0)
    si = lax.broadcasted_iota(jnp.int32, (c, c), 1)
    ltri = (si <= ti).astype(BF16)

    def stack(x):
        return jnp.where(keep, jnp.concatenate([x, x], axis=0), 0.0).astype(BF16)

    cs = range(n_chunks)
    each = lambda fn, *lists: [fn(*xs) for xs in zip(*lists)]
    split = lambda x: [x[ci * c:(ci + 1) * c, :] for ci in cs]
    r, lw, k, v, al, be = (split(x) for x in (r_all, lw_all, k_all, v_all, al_all, be_all))
    cl = each(lambda x: _split_dot_rhs(ltri, x), lw)
    e_pos = each(jnp.exp, cl)
    e_neg = each(lambda x: jnp.exp(-x), cl)
    e_prev = each(lambda x, y: jnp.exp(x - y), cl, lw)
    wc = each(lambda x: x[c - 1:c, :], e_pos)
    kt = each(jnp.multiply, k, e_neg)
    bt = each(jnp.multiply, be, e_neg)
    la = each(lambda x, y: stack(x * y), al, e_prev)
    lr = each(lambda x, y: stack(x * y), r, e_pos)
    rb, rk, vs = each(stack, bt), each(stack, kt), each(stack, v)
    rbw = each(lambda x, y: stack(x * y), bt, wc)
    rkw = each(lambda x, y: stack(x * y), kt, wc)
    nt_dot = lambda x, y: _dg(x, y, NT_DIMS)
    a_ab = each(lambda x, y: jnp.where(strict, nt_dot(x, y), 0.0), la, rb)
    a_ak = each(lambda x, y: jnp.where(strict, nt_dot(x, y), 0.0).astype(BF16), la, rk)
    a_rb = each(lambda x, y: jnp.where(incl, nt_dot(x, y), 0.0).astype(BF16), lr, rb)
    a_rk = each(lambda x, y: jnp.where(incl, nt_dot(x, y), 0.0).astype(BF16), lr, rk)
    dec_softmax()
    p = each(lambda x: eye + x, a_ab)
    pw = a_ab
    for level in range(int(math.log2(c)) - 1):
        pwb = each(lambda x: x.astype(BF16), pw)
        pw = each(_dot, pwb, pwb)
        p = each(lambda x, y: x + _dot(x.astype(BF16), y.astype(BF16)), p, pw)
        if level == 1:
            dec_values()
    tb = each(lambda x: x.astype(BF16), p)
    ua = each(lambda x, y: _dot(x, y).astype(BF16), tb, la)
    akv = each(lambda x, y: _dot(x, y).astype(BF16), a_ak, vs)
    uv = each(lambda x, y: _dot(x, y).astype(BF16), tb, akv)
    lr2 = each(lambda x, y, z: (x.astype(F32) + _dot(y, z)).astype(BF16), lr, a_rb, ua)
    op = each(lambda x, y, z, w: _dot(x, y) + _dot(z, w), a_rb, uv, a_rk, vs)
    m2 = each(lambda x, y: _dg(x, y, TN_DIMS).astype(BF16), ua, rbw)
    n2 = each(lambda x, y, z, w: _dg(jnp.concatenate([x, y], axis=0),
                                     jnp.concatenate([z, w], axis=0), TN_DIMS), uv, vs, rbw, rkw)
    for ci in cs:
        lr_o[ci] = lr2[ci]
        op_o[pl.ds(ci * c, c), :] = op[ci][:c] + op[ci][c:]
        m_o[ci] = m2[ci]
        n_o[ci] = n2[ci]
        wc_o[ci] = wc[ci]

    @pl.when(step % dec.sps == dec.sps - 1)
    def _():
        k_self = jnp.concatenate([ks_ref[...], ks_ref[...]], axis=0).astype(BF16).astype(F32)
        v_self = jnp.concatenate([vs_ref[...], vs_ref[...]], axis=0).astype(BF16).astype(F32)
        s_self = jnp.sum(qm.astype(F32) * k_self, axis=-1, keepdims=True)
        m_run = m_scr[...]
        m_fin = jnp.maximum(m_run, s_self)
        c_fin = jnp.exp2(m_run - m_fin)
        p_self = jnp.exp2(s_self - m_fin)
        o = (acc_scr[...] * c_fin + p_self * v_self) / (l_scr[...] * c_fin + p_self)
        lam = _lambda(lq1[...], lk1[...], lq2[...], lk2[...], dec.lam0)
        od_o[...] = _diff_finish(o[:A_HEADS], o[A_HEADS:], lam, g_ref[...], dec.lam0)


def _rwkv_state_kernel(m_ref, n_ref, lr_ref, op_ref, wc_ref, o_ref, s_ref, s_scr, *, n_chunks, pairs):
    c = RWKV_CHUNK

    @pl.when(pl.program_id(1) == 0)
    def _():
        s_scr[...] = jnp.zeros_like(s_scr)

    def body(ci, carry):
        sl = pl.ds(pl.multiple_of(ci * c, c), c)
        outs = []
        for p in range(pairs):
            s = s_scr[p]
            sb = s.astype(BF16)
            o = _dg(lr_ref[ci, p], sb, NT_DIMS)
            outs.append(o[:c] + o[c:])
            s_scr[p] = s * wc_ref[ci, p] + _dot(sb, m_ref[ci, p]) + n_ref[ci, p]
        o_ref[sl, :] = jnp.concatenate(outs, axis=1) + op_ref[sl, :]
        return carry

    lax.fori_loop(0, n_chunks, body, 0)
    s_ref[...] = s_scr[...]


DecodePlan = collections.namedtuple("DecodePlan", "pps sps n_pages page lam0")


def _rwkv_scan(sh, pw, batch, seq_len, tt_chunk, tt_state, qm, k_self, v_self, cache_k, cache_v,
               page_table, aw, lam0):
    rows = sh.shape[0]
    rw = pw["w0"].shape[1]
    pairs = rw // LANES
    c = RWKV_CHUNK
    n_all = rows // c
    nb = tt_chunk // c
    gcols = rw // LANES
    n_steps = (rows // tt_chunk) * pairs
    dbatch, n_pages = page_table.shape
    n_phys, page = cache_k.shape[0], cache_k.shape[1]
    assert (dbatch * n_pages) % n_steps == 0, "decode pages must spread evenly over the chunk grid"
    pps = dbatch * n_pages // n_steps
    assert n_pages % pps == 0
    dec = DecodePlan(pps=pps, sps=n_pages // pps, n_pages=n_pages, page=page, lam0=lam0)
    ck = cache_k.reshape(n_phys, page * A_HEADS, LANES)
    cv = cache_v.reshape(n_phys, page * A_HEADS, LANES)
    nrow = 2 * A_HEADS

    tile = pl.BlockSpec((tt_chunk, LANES), lambda i, p, *_: (i, p))
    mat = pl.BlockSpec((nb, None, LANES, LANES), lambda i, p, *_: (i, p, 0, 0))
    win = lambda g: pl.BlockSpec((tt_chunk, LANES), lambda i, p, *_: (i, g * gcols + p))
    lora = pl.BlockSpec((tt_chunk, LANES), lambda i, p, *_: (i, 3 * gcols))
    trow = lambda i: jnp.maximum(i * (tt_chunk // 8) - 1, 0)
    twin = lambda g: pl.BlockSpec((8, LANES), lambda i, p, *_: (trow(i), g * gcols + p))
    tlora = pl.BlockSpec((8, LANES), lambda i, p, *_: (trow(i), 3 * gcols))
    mwin = lambda g: pl.BlockSpec((1, LANES), lambda i, p, *_: (0, g * gcols + p))
    mlora = pl.BlockSpec((1, LANES), lambda i, p, *_: (0, 3 * gcols))
    vec = pl.BlockSpec((1, LANES), lambda i, p, *_: (0, p))
    up = pl.BlockSpec((LANES, LANES), lambda i, p, *_: (0, p))
    per_seq = lambda r: pl.BlockSpec((None, r, LANES), lambda i, p, *_: ((i * pairs + p) // dec.sps, 0, 0))
    cvec = lambda n: pl.BlockSpec((1, n), lambda i, p, *_: (0, 0))
    hbm = pl.BlockSpec(memory_space=pl.ANY)
    grid_spec = pltpu.PrefetchScalarGridSpec(
        num_scalar_prefetch=1, grid=(rows // tt_chunk, pairs),
        in_specs=[win(0), win(1), win(2), lora, twin(0), twin(1), twin(2), tlora,
                  mwin(0), mwin(1), mwin(2), mlora, vec, up, vec, up, vec, vec, vec,
                  per_seq(nrow), pl.BlockSpec((nrow, 1), lambda i, p, *_: (0, 0)),
                  per_seq(A_HEADS), per_seq(A_HEADS),
                  cvec(A_QK), cvec(A_QK), cvec(A_QK), cvec(A_QK), cvec(A_V), hbm, hbm],
        out_specs=[mat, mat, pl.BlockSpec((nb, None, 2 * c, LANES), lambda i, p, *_: (i, p, 0, 0)), tile,
                   pl.BlockSpec((nb, None, 1, LANES), lambda i, p, *_: (i, p, 0, 0)), tile,
                   per_seq(A_HEADS)],
        scratch_shapes=[pltpu.VMEM((2, pps, page * A_HEADS, LANES), F32),
                        pltpu.VMEM((2, pps, page * A_HEADS, LANES), F32),
                        pltpu.SemaphoreType.DMA((2, 2, pps)),
                        pltpu.VMEM((nrow, 1), F32), pltpu.VMEM((nrow, 1), F32),
                        pltpu.VMEM((nrow, LANES), F32)])
    m_, n_, lr_, op_, wc_, bonus, o_dec = pl.pallas_call(
        functools.partial(_rwkv_chunk_kernel, n_chunks=nb, tiles_per_seq=seq_len // tt_chunk,
                          pairs=pairs, dec=dec),
        out_shape=[jax.ShapeDtypeStruct((n_all, pairs, LANES, LANES), BF16),
                   jax.ShapeDtypeStruct((n_all, pairs, LANES, LANES), F32),
                   jax.ShapeDtypeStruct((n_all, pairs, 2 * c, LANES), BF16),
                   jax.ShapeDtypeStruct((rows, rw), F32),
                   jax.ShapeDtypeStruct((n_all, pairs, 1, LANES), F32),
                   jax.ShapeDtypeStruct((rows, rw), F32),
                   jax.ShapeDtypeStruct((dbatch, A_HEADS, LANES), F32)],
        grid_spec=grid_spec,
        compiler_params=_params("arbitrary", "arbitrary"),
        name="rwkv_chunk_decode",
    )(page_table, sh, sh, sh, sh, sh, sh, sh, sh, pw["mu"], pw["mu"], pw["mu"], pw["mu"],
      pw["w0"], pw["w2p"], pw["a0"], pw["a2p"], pw["k_k"], pw["k_a"], pw["r_k"],
      qm, aw["slopes16"], k_self, v_self, aw["lq1"], aw["lk1"], aw["lq2"], aw["lk2"], aw["subln_g"],
      ck, cv)

    nt = seq_len // tt_state
    ns = tt_state // c
    smat = lambda rws: pl.BlockSpec((ns, pairs, rws, LANES), lambda b, i: (b * nt + i, 0, 0, 0))
    wide = pl.BlockSpec((tt_state, rw), lambda b, i: (b * nt + i, 0))
    o, s = pl.pallas_call(
        functools.partial(_rwkv_state_kernel, n_chunks=ns, pairs=pairs),
        out_shape=[jax.ShapeDtypeStruct((rows, rw), F32),
                   jax.ShapeDtypeStruct((batch, pairs, LANES, LANES), F32)],
        grid=(batch, nt),
        in_specs=[smat(LANES), smat(LANES), smat(2 * c), wide, smat(1)],
        out_specs=[wide, pl.BlockSpec((None, pairs, LANES, LANES), lambda b, i: (b, 0, 0, 0))],
        scratch_shapes=[pltpu.VMEM((pairs, LANES, LANES), F32)],
        compiler_params=_params("parallel", "arbitrary"),
        name="rwkv_state",
    )(m_, n_, lr_, op_, wc_)
    s = s.reshape(batch, pairs, 2, R_HEAD, 2, R_HEAD)
    s = jnp.stack([s[:, :, 0, :, 0, :], s[:, :, 1, :, 1, :]], axis=2)
    return o, bonus, s.reshape(batch, 2 * pairs, R_HEAD, R_HEAD), o_dec


def _rwkv_step_kernel(s_ref, r_ref, lw_ref, k_ref, al_ref, be_ref, vcol_ref, o_ref, so_ref):
    s = s_ref[...]
    sa = jnp.sum(s * al_ref[...], axis=-1, keepdims=True)
    s_new = s * jnp.exp(lw_ref[...]) + sa * be_ref[...] + vcol_ref[...] * k_ref[...]
    so_ref[...] = s_new
    o_ref[...] = jnp.sum(s_new * r_ref[...], axis=-1, keepdims=True)


def _rwkv_step(state, r, lw, k, v, al, be):
    b, h, n, _ = state.shape
    heads = b * h
    hb = _tile(heads, 64)
    vec = lambda x: x.reshape(heads, 1, n)
    vspec = pl.BlockSpec((hb, 1, n), lambda i: (i, 0, 0))
    cspec = pl.BlockSpec((hb, n, 1), lambda i: (i, 0, 0))
    sspec = pl.BlockSpec((hb, n, n), lambda i: (i, 0, 0))
    o, s_new = pl.pallas_call(
        _rwkv_step_kernel,
        out_shape=[jax.ShapeDtypeStruct((heads, n, 1), F32), jax.ShapeDtypeStruct((heads, n, n), F32)],
        grid=(heads // hb,),
        in_specs=[sspec, vspec, vspec, vspec, vspec, vspec, cspec],
        out_specs=[cspec, sspec],
        compiler_params=_params("parallel"), name="rwkv_step",
    )(state.reshape(heads, n, n), vec(r), vec(lw), vec(k), vec(al), vec(be), v.reshape(heads, n, 1))
    return o.reshape(b, h * n), s_new.reshape(state.shape)


def _lambda(lq1, lk1, lq2, lk2, lam0):
    t1 = jnp.sum(lq1 * lk1, axis=-1, keepdims=True)
    t2 = jnp.sum(lq2 * lk2, axis=-1, keepdims=True)
    return jnp.exp(t1) - jnp.exp(t2) + lam0


def _diff_finish(o1, o2, lam, subln_g, lam0):
    o = o1 - lam * o2
    y = o * lax.rsqrt(jnp.mean(o * o, axis=-1, keepdims=True) + RMS_EPS)
    return y * subln_g * (1.0 - lam0)


ATTN_GROUP = 256
SUM_ROWS = 16
ATTN_KEYS = 1024
ATTN_AHEAD = 4
ALIBI_SPLIT = 256


def _attn_prompt_kernel(slopes_ref, q_ref, k_ref, v_ref, lq1, lk1, lq2, lk2, g_ref, o_ref,
                        kb, vt, *, tq, lam0):
    h = pl.program_id(1)
    qi = pl.program_id(2)
    seq_len = k_ref.shape[0]

    @pl.when(qi == 0)
    def _():
        slope = slopes_ref[h]
        pos = lax.broadcasted_iota(jnp.int32, (seq_len, LANES), 0)
        lane = lax.broadcasted_iota(jnp.int32, (seq_len, LANES), 1)
        lo = (pos % ALIBI_SPLIT).astype(F32) * slope
        hi = (pos - pos % ALIBI_SPLIT).astype(F32) * slope
        kb[:, :LANES] = k_ref[...].astype(BF16)
        n_alibi = 2 * len(LOG2E_TERMS)
        kb[:, LANES:] = jnp.where(lane >= n_alibi, 0.0, jnp.where(lane % 2 == 0, lo, hi)).astype(BF16)
        for c0 in range(0, seq_len, tq):
            vt[:LANES, c0:c0 + tq] = v_ref[c0:c0 + tq, :].T.astype(BF16)
        vt[LANES:, :] = jnp.ones((SUM_ROWS, seq_len), BF16)

    lane = lax.broadcasted_iota(jnp.int32, (tq, LANES), 1)
    q = q_ref[...]
    ones = jnp.zeros((tq, LANES), F32)
    for i, term in enumerate(LOG2E_TERMS):
        ones = jnp.where(lane // 2 == i, term, ones)
    ones = ones.astype(BF16)
    zero = jnp.zeros_like(q)
    q_aug = jnp.concatenate([
        jnp.concatenate([jnp.where(lane < A_QK, q, zero), ones], axis=1),
        jnp.concatenate([jnp.where(lane >= A_QK, q, zero), ones], axis=1)], axis=0)
    gw = ATTN_GROUP
    groups = range(2 * tq // gw)
    q_grp = [q_aug[g * gw:(g + 1) * gw, :] for g in groups]

    ksub = min(ATTN_KEYS, tq)

    def block(j, carry, diagonal):
        m, acc = (list(x) for x in carry)
        steps = []
        for a in range(tq // ksub):
            for g in groups:
                q0 = (g * gw) % tq
                rows = min(ksub, q0 + gw - a * ksub) if diagonal else ksub
                if rows > 0:
                    steps.append((a, g, rows, diagonal and (a * ksub + rows - 1 > q0)))

        def scores(a, g, rows, masked):
            ks = pl.ds(pl.multiple_of(j * tq + a * ksub, ksub), rows)
            s = _dg(kb[ks, :], q_grp[g], NT_DIMS)
            if masked:
                r_i = lax.broadcasted_iota(jnp.int32, (rows, gw), 0) + a * ksub
                c_i = lax.broadcasted_iota(jnp.int32, (rows, gw), 1) + (g * gw) % tq
                s = jnp.where(r_i <= c_i, s, NEG_INF)
            return s

        ahead = [scores(*st) for st in steps[:ATTN_AHEAD]]
        for idx, (a, g, rows, _) in enumerate(steps):
            if idx + ATTN_AHEAD < len(steps):
                ahead.append(scores(*steps[idx + ATTN_AHEAD]))
            s = ahead.pop(0)
            ks = pl.ds(pl.multiple_of(j * tq + a * ksub, ksub), rows)
            m_new = jnp.maximum(m[g], jnp.max(s, axis=0, keepdims=True))
            p = jnp.exp2(s - m_new)
            corr = jnp.exp2(m[g] - m_new)
            acc[g] = acc[g] * corr + _dot(vt[:, ks], p.astype(BF16))
            m[g] = m_new
        return tuple(m), tuple(acc)

    init = (tuple(jnp.full((1, gw), NEG_INF, F32) for _ in groups),
            tuple(jnp.zeros((LANES + SUM_ROWS, gw), F32) for _ in groups))
    carry = lax.fori_loop(0, qi, lambda j, cr: block(j, cr, False), init)
    _, acc = block(qi, carry, True)

    o = jnp.concatenate([a[:LANES] / a[LANES:LANES + 1] for a in acc], axis=1).T
    lam = _lambda(lq1[...], lk1[...], lq2[...], lk2[...], lam0)
    o_ref[...] = _diff_finish(o[:tq], o[tq:], lam, g_ref[...], lam0)


def _attn_prompt(q, k, v, aw, batch, seq_len, tq, lam0):
    rows, width = k.shape
    nq = seq_len // tq
    qspec = pl.BlockSpec((tq, LANES), lambda b, h, i, *_: (b * nq + i, h))
    kvspec = pl.BlockSpec((seq_len, LANES), lambda b, h, i, *_: (b, h))
    vec = lambda n: pl.BlockSpec((1, n), lambda b, h, i, *_: (0, 0))
    grid_spec = pltpu.PrefetchScalarGridSpec(
        num_scalar_prefetch=1, grid=(batch, A_HEADS, nq),
        in_specs=[qspec, kvspec, kvspec, vec(A_QK), vec(A_QK), vec(A_QK), vec(A_QK), vec(A_V)],
        out_specs=qspec,
        scratch_shapes=[pltpu.VMEM((seq_len, 2 * LANES), BF16), pltpu.VMEM((LANES + SUM_ROWS, seq_len), BF16)])
    return pl.pallas_call(
        functools.partial(_attn_prompt_kernel, tq=tq, lam0=lam0),
        out_shape=jax.ShapeDtypeStruct((rows, width), F32),
        grid_spec=grid_spec,
        compiler_params=_params("parallel", "parallel", "arbitrary"),
        name="attn_prompt",
    )(aw["slopes"], q, k, v, aw["lq1"], aw["lk1"], aw["lq2"], aw["lk2"], aw["subln_g"])


def _merge_kernel(or_ref, bo_ref, zr_ref, oa_ref, za_ref, ga_ref, gr_ref,
                  lng_ref, lnb_ref, wr_ref, wa_ref, o_ref):
    o_r = or_ref[...]
    inv_n = 1.0 / R_HEAD
    mean = _head_sums(o_r) * inv_n
    d = o_r - mean
    var = _head_sums(d * d) * inv_n
    o_r = d * lax.rsqrt(var + GN_EPS) * lng_ref[...] + lnb_ref[...] + bo_ref[...]
    f32 = lambda ref: ref[...].astype(F32)
    x_r = (o_r * _silu(f32(zr_ref))).astype(BF16)
    x_a = (oa_ref[...] * _silu(f32(za_ref))).astype(BF16)
    y_r = _dot(x_r, wr_ref[...])
    y_a = _dot(x_a, wa_ref[...])
    o_ref[...] = (_sigmoid(f32(ga_ref)) * y_a + _sigmoid(f32(gr_ref)) * y_r).astype(o_ref.dtype)


def _merge(o_r, bonus, o_a, zr, zag, mw, tm):
    rows, rw = o_r.shape
    d = mw["w_r_up"].shape[1]
    tile = pl.BlockSpec((tm, rw), lambda i: (i, 0))
    gate = lambda col0: pl.BlockSpec((pl.Element(tm), pl.Element(d)),
                                     lambda i: (pl.multiple_of(i * tm, 8), col0))
    const = lambda shape: pl.BlockSpec(shape, lambda i: (0,) * len(shape))
    return pl.pallas_call(
        _merge_kernel,
        out_shape=jax.ShapeDtypeStruct((rows, d), BF16),
        grid=(rows // tm,),
        in_specs=[tile, tile, tile, tile, tile, gate(rw), gate(rw + d),
                  const((1, rw)), const((1, rw)),
                  const(mw["w_r_up"].shape), const(mw["w_a_up"].shape)],
        out_specs=pl.BlockSpec((tm, d), lambda i: (i, 0)),
        compiler_params=_params("parallel"), name="merge",
    )(o_r, bonus, zr, o_a, zag, zag, zag, mw["lnx_g"], mw["lnx_b"], mw["w_r_up"], mw["w_a_up"])


def _out_kernel(x_ref, m_ref, p_ref, wo_ref, wp_ref, wg_ref, g_ref, o_ref):
    h = x_ref[...] + _dot(m_ref[...], wo_ref[...])
    ple = _dot(p_ref[...].astype(BF16), wp_ref[...])
    h = h + ple * _sigmoid(_dot(h.astype(BF16), wg_ref[...]))
    y = h * lax.rsqrt(jnp.mean(h * h, axis=-1, keepdims=True) + RMS_EPS)
    o_ref[...] = y * g_ref[...]


def _out_stage(x, m, p, ow, tm):
    rows, d = x.shape
    pd = p.shape[1]
    const = lambda shape: pl.BlockSpec(shape, lambda i: (0,) * len(shape),
                                       pipeline_mode=pl.Buffered(1))
    return pl.pallas_call(
        _out_kernel,
        out_shape=jax.ShapeDtypeStruct((rows, d), F32),
        grid=(rows // tm,),
        in_specs=[pl.BlockSpec((tm, d), lambda i: (i, 0)), pl.BlockSpec((tm, d), lambda i: (i, 0)),
                  pl.BlockSpec((tm, pd), lambda i: (i, 0)),
                  const((d, d)), const((pd, d)), const((d, d)), const((1, d))],
        out_specs=pl.BlockSpec((tm, d), lambda i: (i, 0)),
        compiler_params=_params("parallel"), name="out_stage",
    )(x, m, p, ow["w_out"], ow["w_ple"], ow["w_ple_gate"], ow["norm_final_g"])


def _tile(n, pref):
    return pref if n % pref == 0 else n


def kernel(x_prompt, x_sample, cache_k, cache_v, state_wkv, state_shift, page_table, p_prompt,
           p_sample, norm_in_g, w_in, mu_shift, w0, w2, a0, a2, k_k, k_a, r_k, lnx_g, lnx_b,
           w_rwkv_up, lambda_q1, lambda_k1, lambda_q2, lambda_k2, subln_g, w_attn_up, w_out,
           w_ple, w_ple_gate, norm_final_g):
    depth = w_in.shape[0]
    assert depth == 1, "single-layer trunk"
    layer = 0
    lam0 = 0.8 - 0.6 * math.exp(-0.3 * layer)
    bp, tp, d = x_prompt.shape
    bs, ts, _ = x_sample.shape
    assert ts == 1
    rw = w0.shape[1]
    sw = mu_shift.shape[1]
    aqk = A_HEADS * 2 * A_QK
    aw_ = A_HEADS * A_V
    row = lambda x: x.reshape(1, -1)

    w_all = w_in[layer].astype(BF16)
    c_zr, c_q, c_k, c_v, c_za = sw, sw + rw, sw + rw + aqk, sw + rw + 2 * aqk, sw + rw + 2 * aqk + aw_
    n_zag = w_all.shape[1] - c_za

    lora = w2.shape[1]
    zpad = jnp.zeros((LANES - lora, rw), BF16)
    pw = dict(mu=row(mu_shift[layer]), w0=row(w0[layer]), a0=row(a0[layer]), k_k=row(k_k[layer]),
              k_a=row(k_a[layer]), r_k=row(r_k[layer]),
              w2p=jnp.concatenate([w2[layer].astype(BF16), zpad], axis=0),
              a2p=jnp.concatenate([zpad, a2[layer].astype(BF16)], axis=0))
    slopes = 2.0 ** (-8.0 * jnp.arange(1, A_HEADS + 1, dtype=F32) / A_HEADS)
    aw = dict(slopes=slopes, slopes16=jnp.tile(slopes, 2).reshape(2 * A_HEADS, 1),
              lq1=row(lambda_q1[layer]), lk1=row(lambda_k1[layer]), lq2=row(lambda_q2[layer]),
              lk2=row(lambda_k2[layer]), subln_g=row(subln_g[layer]))
    mw = dict(lnx_g=row(lnx_g[layer]), lnx_b=row(lnx_b[layer]),
              w_r_up=w_rwkv_up[layer].astype(BF16), w_a_up=w_attn_up[layer].astype(BF16))
    ow = dict(w_out=w_out[layer].astype(BF16), w_ple=w_ple[layer].astype(BF16),
              w_ple_gate=w_ple_gate[layer].astype(BF16), norm_final_g=row(norm_final_g))

    def project(x2d, tm):
        xn, sh = _rms_proj(x2d, norm_in_g[layer], w_all, sw, min(tm, 512))
        zr = _matmul(xn, w_all, c_zr, rw, BF16, tm, 1024, name="proj_zr")
        q = _matmul(xn, w_all, c_q, aqk, BF16, tm, 1024, scale=A_QK ** -0.5 * LOG2E, name="proj_q")
        k = _matmul(xn, w_all, c_k, aqk, F32, tm, 1024, name="proj_k")
        v = _matmul(xn, w_all, c_v, aw_, F32, tm, 1024, name="proj_v")
        zag = _matmul(xn, w_all, c_za, n_zag, BF16, tm, 1024, name="proj_gates")
        return sh, zr, q, k, v, zag

    xp = x_prompt.reshape(bp * tp, d)
    xs = x_sample.reshape(bs, d)
    sh, zr, q, k, v, zag = project(xp, _tile(bp * tp, 1024))
    sh_s, zr_s, q_s, k_s, v_s, zag_s = project(xs, bs)
    q_h = q_s.reshape(bs, 1, A_HEADS, 2, A_QK)
    sel = jnp.eye(2, dtype=q_s.dtype).reshape(1, 2, 1, 2, 1)
    qm = (q_h * sel).reshape(bs, 2 * A_HEADS, 2 * A_QK)

    o_r, bonus, wkv_p, o_dec = _rwkv_scan(
        sh, pw, bp, tp, _tile(tp, 512), _tile(tp, 512), qm, k_s.reshape(bs, A_HEADS, 2 * A_QK),
        v_s.reshape(bs, A_HEADS, A_V), cache_k[layer], cache_v[layer], page_table, aw, lam0)
    o_a = _attn_prompt(q, k, v, aw, bp, tp, _tile(tp, 1024), lam0)
    m = _merge(o_r, bonus, o_a, zr, zag, mw, _tile(bp * tp, 256))
    y_p = _out_stage(xp, m, p_prompt[layer].reshape(bp * tp, -1), ow, _tile(bp * tp, 512))
    y_prompt = y_p.reshape(bp, tp, d)
    k_prompt = k.reshape(1, bp, tp, A_HEADS, 2 * A_QK)
    v_prompt = v.reshape(1, bp, tp, A_HEADS, A_V)
    shift_prompt = sh.reshape(bp, tp, sw)[:, -1][None]

    r_, lw_, k_, v_, al_, be_, bonus_s = _rwkv_prep_sample(sh_s, state_shift[layer], pw)
    o_r, wkv_s = _rwkv_step(state_wkv[layer], r_, lw_, k_, v_, al_, be_)
    m = _merge(o_r, bonus_s, o_dec.reshape(bs, aw_), zr_s, zag_s, mw, bs)
    y_s = _out_stage(xs, m, p_sample[layer].reshape(bs, -1), ow, bs)
    y_sample = y_s.reshape(bs, ts, d)
    k_sample = k_s.reshape(1, bs, ts, A_HEADS, 2 * A_QK)
    v_sample = v_s.reshape(1, bs, ts, A_HEADS, A_V)

    return (y_prompt, y_sample, k_prompt, v_prompt, k_sample, v_sample,
            wkv_p[None], wkv_s[None], shift_prompt, sh_s[None])
```

```python
import collections
import functools
import math
import struct

import jax
import jax.numpy as jnp
from jax import lax
from jax.experimental import pallas as pl
from jax.experimental.pallas import tpu as pltpu

F32 = jnp.float32
BF16 = jnp.bfloat16

LANES = 128
A_HEADS = 8
A_QK = 64
A_V = 2 * A_QK
R_HEAD = 64
RMS_EPS = 1e-5
GN_EPS = 64e-5
NEG_INF = -1e30
RWKV_CHUNK = 64
VMEM_LIMIT = 56 * 1024 * 1024


def _f32(x):
    return struct.unpack("<f", struct.pack("<f", x))[0]


def _bf16_terms(x):
    terms, rest = [], _f32(x)
    for _ in range(3):
        bits = struct.unpack("<I", struct.pack("<f", rest))[0] & 0xFFFF0000
        terms.append(struct.unpack("<f", struct.pack("<I", bits))[0])
        rest = _f32(rest - terms[-1])
    assert rest == 0.0
    return terms


LOG2E = _f32(1.0 / math.log(2.0))
LOG2E_TERMS = _bf16_terms(LOG2E)

NT_DIMS = (((1,), (1,)), ((), ()))
TN_DIMS = (((0,), (0,)), ((), ()))


def _params(*sem):
    return pltpu.CompilerParams(dimension_semantics=sem, vmem_limit_bytes=VMEM_LIMIT)


def _dot(a, b):
    return jnp.dot(a, b, preferred_element_type=F32)


def _dg(a, b, dims):
    return lax.dot_general(a, b, dims, preferred_element_type=F32)


def _split_dot_rhs(a_bf16, x):
    hi = x.astype(BF16)
    lo = (x - hi.astype(F32)).astype(BF16)
    return _dot(a_bf16, hi) + _dot(a_bf16, lo)


def _sigmoid(x):
    return 1.0 / (1.0 + jnp.exp(-x))


def _silu(x):
    return x * _sigmoid(x)


def _head_sums(x):
    width = x.shape[1]
    win = 2 * LANES if width % (2 * LANES) == 0 else LANES
    i = lax.broadcasted_iota(jnp.int32, (win, win), 0)
    j = lax.broadcasted_iota(jnp.int32, (win, win), 1)
    gmat = ((i // R_HEAD) == (j // R_HEAD)).astype(BF16)
    hi = x.astype(BF16)
    lo = (x - hi.astype(F32)).astype(BF16)
    parts = []
    for c0 in range(0, width, win):
        h, l = hi[:, c0:c0 + win], lo[:, c0:c0 + win]
        if win == LANES:
            parts.append(_dot(jnp.concatenate([h, l], axis=1), jnp.concatenate([gmat, gmat], axis=0)))
        else:
            parts.append(_dot(h, gmat) + _dot(l, gmat))
    return jnp.concatenate(parts, axis=1)


def _rms_proj_kernel(x_ref, g_ref, w_ref, xn_ref, o_ref):
    x = x_ref[...]
    y = x * lax.rsqrt(jnp.mean(x * x, axis=-1, keepdims=True) + RMS_EPS)
    xn = (y * g_ref[...]).astype(BF16)
    xn_ref[...] = xn
    o_ref[...] = _dot(xn, w_ref[...])


def _rms_proj(x, g, w, n, tm):
    m, d = x.shape
    return pl.pallas_call(
        _rms_proj_kernel,
        out_shape=[jax.ShapeDtypeStruct((m, d), BF16), jax.ShapeDtypeStruct((m, n), F32)],
        grid=(m // tm,),
        in_specs=[pl.BlockSpec((tm, d), lambda i: (i, 0)),
                  pl.BlockSpec((1, d), lambda i: (0, 0)),
                  pl.BlockSpec((pl.Element(d), pl.Element(n)), lambda i: (0, 0),
                               pipeline_mode=pl.Buffered(1))],
        out_specs=[pl.BlockSpec((tm, d), lambda i: (i, 0)), pl.BlockSpec((tm, n), lambda i: (i, 0))],
        compiler_params=_params("parallel"),
        name="rms_proj_shift",
    )(x, g.reshape(1, d), w)


def _mm_kernel(a_ref, b_ref, o_ref, *, scale):
    acc = _dot(a_ref[...], b_ref[...])
    if scale != 1.0:
        acc = acc * scale
    o_ref[...] = acc.astype(o_ref.dtype)


def _matmul(a, b, col0, n, out_dtype, tm, tn, scale=1.0, name="matmul"):
    m, k = a.shape
    mode = dict(pipeline_mode=pl.Buffered(1)) if tn == n else {}
    return pl.pallas_call(
        functools.partial(_mm_kernel, scale=scale),
        out_shape=jax.ShapeDtypeStruct((m, n), out_dtype),
        grid=(m // tm, n // tn),
        in_specs=[pl.BlockSpec((tm, k), lambda i, j: (i, 0)),
                  pl.BlockSpec((pl.Element(k), pl.Element(tn)),
                               lambda i, j: (0, pl.multiple_of(col0 + j * tn, LANES)), **mode)],
        out_specs=pl.BlockSpec((tm, tn), lambda i, j: (i, j)),
        compiler_params=_params("parallel", "parallel"),
        name=name,
    )(a, b)


def _rwkv_token_math(sh3, lora, prev3, prev_lora, mu3, mu_lora, w0, w2p, a0, a2p, k_k, k_a, r_k):
    lerp = lambda x, p, mu: x + mu * (p - x)
    r, kr, vr = (lerp(x, p, mu) for x, p, mu in zip(sh3, prev3, mu3))
    lo = lerp(lora, prev_lora, mu_lora)
    z = w0 + _dot(jnp.tanh(lo).astype(BF16), w2p)
    softplus = jnp.maximum(-z, 0.0) + jnp.log1p(jnp.exp(-jnp.abs(z)))
    logw = -jnp.exp(-softplus - 0.5)
    a = _sigmoid(a0 + _dot(lo.astype(BF16), a2p))
    kk = kr * k_k
    norm = jnp.maximum(jnp.sqrt(_head_sums(kk * kk)), 1e-12)
    kk = kk / norm
    kmod = kr * (1.0 + (a - 1.0) * k_a)
    bonus = _head_sums(r * kmod * r_k) * vr
    return r, logw, kmod, vr, -kk, kk * a, bonus


def _rwkv_prep_sample_kernel(sh_ref, prev_ref, mu_ref, w0_ref, w2_ref, a0_ref, a2_ref,
                             kk_ref, ka_ref, rk_ref, r_o, lw_o, k_o, v_o, al_o, be_o, bo_o):
    rw = w0_ref.shape[1]
    win3 = lambda ref: tuple(ref[:, i * rw:(i + 1) * rw] for i in range(3))
    lo = lambda ref: ref[:, 3 * rw:]
    outs = _rwkv_token_math(win3(sh_ref), lo(sh_ref), win3(prev_ref), lo(prev_ref), win3(mu_ref),
                            lo(mu_ref), w0_ref[...], w2_ref[...], a0_ref[...], a2_ref[...],
                            kk_ref[...], ka_ref[...], rk_ref[...])
    for ref, val in zip((r_o, lw_o, k_o, v_o, al_o, be_o, bo_o), outs):
        ref[...] = val


def _rwkv_prep_sample(sh, prev, pw):
    rows, sw = sh.shape
    rw = pw["w0"].shape[1]
    const = lambda shape: pl.BlockSpec(shape, lambda *_: (0,) * len(shape))
    vec = const((1, rw))
    return pl.pallas_call(
        _rwkv_prep_sample_kernel, out_shape=[jax.ShapeDtypeStruct((rows, rw), F32)] * 7, grid=(1,),
        in_specs=[const((rows, sw)), const((rows, sw)), const((1, sw)), vec, const(pw["w2p"].shape), vec,
                  const(pw["a2p"].shape), vec, vec, vec],
        out_specs=[const((rows, rw))] * 7,
        compiler_params=_params("arbitrary"), name="rwkv_prep_sample",
    )(sh, prev, pw["mu"], pw["w0"], pw["w2p"], pw["a0"], pw["a2p"], pw["k_k"], pw["k_a"], pw["r_k"])


def _rwkv_chunk_kernel(pt_ref, r_sh, k_sh, v_sh, lo_sh, r_tl, k_tl, v_tl, lo_tl, mu_r, mu_k, mu_v, mu_lo,
                       w0_ref, w2_ref, a0_ref, a2_ref, kk_ref, ka_ref, rk_ref,
                       qm_ref, slope_ref, ks_ref, vs_ref, lq1, lk1, lq2, lk2, g_ref, ck_hbm, cv_hbm,
                       m_o, n_o, lr_o, op_o, wc_o, bo_o, od_o,
                       kbuf, vbuf, sem, m_scr, l_scr, acc_scr,
                       *, n_chunks, tiles_per_seq, pairs, dec):
    step = pl.program_id(0) * pairs + pl.program_id(1)
    slot = step % 2

    def page_copies(st, sl):
        seq = st // dec.sps
        first = (st % dec.sps) * dec.pps
        copies = []
        for i in range(dec.pps):
            phys = pt_ref[seq, first + i]
            copies.append(pltpu.make_async_copy(ck_hbm.at[phys], kbuf.at[sl, i], sem.at[sl, 0, i]))
            copies.append(pltpu.make_async_copy(cv_hbm.at[phys], vbuf.at[sl, i], sem.at[sl, 1, i]))
        return copies

    @pl.when(step == 0)
    def _():
        for cp in page_copies(step, slot):
            cp.start()

    @pl.when(step + 1 < pl.num_programs(0) * pairs)
    def _():
        for cp in page_copies(step + 1, 1 - slot):
            cp.start()

    @pl.when(step % dec.sps == 0)
    def _():
        m_scr[...] = jnp.full(m_scr.shape, NEG_INF, F32)
        l_scr[...] = jnp.zeros_like(l_scr)
        acc_scr[...] = jnp.zeros_like(acc_scr)

    for cp in page_copies(step, slot):
        cp.wait()

    nrow = 2 * A_HEADS
    pcols = dec.page * A_HEADS
    qm = qm_ref[...]
    dstate = {}

    def dec_scores():
        slope = slope_ref[...] * LOG2E
        r_i = lax.broadcasted_iota(jnp.int32, (nrow, pcols), 0)
        c_i = lax.broadcasted_iota(jnp.int32, (nrow, pcols), 1)
        valid = (c_i % A_HEADS) == (r_i % A_HEADS)
        tok = (c_i // A_HEADS).astype(F32)
        first = (step % dec.sps) * dec.pps
        scores = []
        for i in range(dec.pps):
            base = ((dec.n_pages - (first + i)) * dec.page).astype(F32)
            s = _dg(qm, kbuf[slot, i].astype(BF16), NT_DIMS) - slope * (base - tok)
            scores.append(jnp.where(valid, s, NEG_INF))
        dstate["scores"] = scores

    def dec_softmax():
        scores = dstate["scores"]
        m_old = m_scr[...]
        m_blk = functools.reduce(jnp.maximum, [jnp.max(s, axis=-1, keepdims=True) for s in scores])
        m_new = jnp.maximum(m_old, m_blk)
        corr = jnp.exp2(m_old - m_new)
        probs = [jnp.exp2(s - m_new) for s in scores]
        l_new = l_scr[...] * corr
        for pr in probs:
            l_new = l_new + jnp.sum(pr, axis=-1, keepdims=True)
        l_scr[...] = l_new
        m_scr[...] = m_new
        dstate["probs"] = [pr.astype(BF16) for pr in probs]
        dstate["corr"] = corr

    def dec_values():
        acc = acc_scr[...] * dstate["corr"]
        for i, pr in enumerate(dstate["probs"]):
            acc = acc + _dot(pr, vbuf[slot, i].astype(BF16))
        acc_scr[...] = acc

    seq_start = pl.program_id(0) % tiles_per_seq == 0
    trow = lax.broadcasted_iota(jnp.int32, r_sh.shape, 0)

    def prev_of(ref, tail_ref):
        first = jnp.where(seq_start, 0.0, tail_ref[7:8, :])
        return jnp.where(trow == 0, first, pltpu.roll(ref[...], 1, axis=0))

    dec_scores()
    r_all, lw_all, k_all, v_all, al_all, be_all, bonus = _rwkv_token_math(
        (r_sh[...], k_sh[...], v_sh[...]), lo_sh[...],
        (prev_of(r_sh, r_tl), prev_of(k_sh, k_tl), prev_of(v_sh, v_tl)), prev_of(lo_sh, lo_tl),
        (mu_r[...], mu_k[...], mu_v[...]), mu_lo[...], w0_ref[...], w2_ref[...], a0_ref[...],
        a2_ref[...], kk_ref[...], ka_ref[...], rk_ref[...])
    bo_o[...] = bonus

    c = RWKV_CHUNK
    two_c = 2 * c
    row = lax.broadcasted_iota(jnp.int32, (two_c, LANES), 0)
    col = lax.broadcasted_iota(jnp.int32, (two_c, LANES), 1)
    keep = (row >= c) == (col >= R_HEAD)
    strict = col < row
    incl = col <= row
    eye = (row == col).astype(F32)
    ti = lax.broadcasted_iota(jnp.int32, (c, c), 0)
    si = lax.broadcasted_iota(jnp.int32, (c, c), 1)
    ltri = (si <= ti).astype(BF16)

    def stack(x):
        return jnp.where(keep, jnp.concatenate([x, x], axis=0), 0.0).astype(BF16)

    cs = range(n_chunks)
    each = lambda fn, *lists: [fn(*xs) for xs in zip(*lists)]
    split = lambda x: [x[ci * c:(ci + 1) * c, :] for ci in cs]
    r, lw, k, v, al, be = (split(x) for x in (r_all, lw_all, k_all, v_all, al_all, be_all))
    cl = each(lambda x: _split_dot_rhs(ltri, x), lw)
    e_pos = each(jnp.exp, cl)
    e_neg = each(lambda x: jnp.exp(-x), cl)
    e_prev = each(lambda x, y: jnp.exp(x - y), cl, lw)
    wc = each(lambda x: x[c - 1:c, :], e_pos)
    kt = each(jnp.multiply, k, e_neg)
    bt = each(jnp.multiply, be, e_neg)
    la = each(lambda x, y: stack(x * y), al, e_prev)
    lr = each(lambda x, y: stack(x * y), r, e_pos)
    rb, rk, vs = each(stack, bt), each(stack, kt), each(stack, v)
    rbw = each(lambda x, y: stack(x * y), bt, wc)
    rkw = each(lambda x, y: stack(x * y), kt, wc)
    nt_dot = lambda x, y: _dg(x, y, NT_DIMS)
    a_ab = each(lambda x, y: jnp.where(strict, nt_dot(x, y), 0.0), la, rb)
    a_ak = each(lambda x, y: jnp.where(strict, nt_dot(x, y), 0.0).astype(BF16), la, rk)
    a_rb = each(lambda x, y: jnp.where(incl, nt_dot(x, y), 0.0).astype(BF16), lr, rb)
    a_rk = each(lambda x, y: jnp.where(incl, nt_dot(x, y), 0.0).astype(BF16), lr, rk)
    dec_softmax()
    p = each(lambda x: eye + x, a_ab)
    pw = a_ab
    for level in range(int(math.log2(c)) - 1):
        pwb = each(lambda x: x.astype(BF16), pw)
        pw = each(_dot, pwb, pwb)
        p = each(lambda x, y: x + _dot(x.astype(BF16), y.astype(BF16)), p, pw)
        if level == 1:
            dec_values()
    tb = each(lambda x: x.astype(BF16), p)
    ua = each(lambda x, y: _dot(x, y).astype(BF16), tb, la)
    akv = each(lambda x, y: _dot(x, y).astype(BF16), a_ak, vs)
    uv = each(lambda x, y: _dot(x, y).astype(BF16), tb, akv)
    lr2 = each(lambda x, y, z: (x.astype(F32) + _dot(y, z)).astype(BF16), lr, a_rb, ua)
    op = each(lambda x, y, z, w: _dot(x, y) + _dot(z, w), a_rb, uv, a_rk, vs)
    m2 = each(lambda x, y: _dg(x, y, TN_DIMS).astype(BF16), ua, rbw)
    n2 = each(lambda x, y, z, w: _dg(jnp.concatenate([x, y], axis=0),
                                     jnp.concatenate([z, w], axis=0), TN_DIMS), uv, vs, rbw, rkw)
    for ci in cs:
        lr_o[ci] = lr2[ci]
        op_o[pl.ds(ci * c, c), :] = op[ci][:c] + op[ci][c:]
        m_o[ci] = m2[ci]
        n_o[ci] = n2[ci]
        wc_o[ci] = wc[ci]

    @pl.when(step % dec.sps == dec.sps - 1)
    def _():
        k_self = jnp.concatenate([ks_ref[...], ks_ref[...]], axis=0).astype(BF16).astype(F32)
        v_self = jnp.concatenate([vs_ref[...], vs_ref[...]], axis=0).astype(BF16).astype(F32)
        s_self = jnp.sum(qm.astype(F32) * k_self, axis=-1, keepdims=True)
        m_run = m_scr[...]
        m_fin = jnp.maximum(m_run, s_self)
        c_fin = jnp.exp2(m_run - m_fin)
        p_self = jnp.exp2(s_self - m_fin)
        o = (acc_scr[...] * c_fin + p_self * v_self) / (l_scr[...] * c_fin + p_self)
        lam = _lambda(lq1[...], lk1[...], lq2[...], lk2[...], dec.lam0)
        od_o[...] = _diff_finish(o[:A_HEADS], o[A_HEADS:], lam, g_ref[...], dec.lam0)


def _rwkv_state_kernel(m_ref, n_ref, lr_ref, op_ref, wc_ref, o_ref, s_ref, s_scr, *, n_chunks, pairs):
    c = RWKV_CHUNK

    @pl.when(pl.program_id(1) == 0)
    def _():
        s_scr[...] = jnp.zeros_like(s_scr)

    def body(ci, carry):
        sl = pl.ds(pl.multiple_of(ci * c, c), c)
        outs = []
        for p in range(pairs):
            s = s_scr[p]
            sb = s.astype(BF16)
            o = _dg(lr_ref[ci, p], sb, NT_DIMS)
            outs.append(o[:c] + o[c:])
            s_scr[p] = s * wc_ref[ci, p] + _dot(sb, m_ref[ci, p]) + n_ref[ci, p]
        o_ref[sl, :] = jnp.concatenate(outs, axis=1) + op_ref[sl, :]
        return carry

    lax.fori_loop(0, n_chunks, body, 0)
    s_ref[...] = s_scr[...]


DecodePlan = collections.namedtuple("DecodePlan", "pps sps n_pages page lam0")


def _rwkv_scan(sh, pw, batch, seq_len, tt_chunk, tt_state, qm, k_self, v_self, cache_k, cache_v,
               page_table, aw, lam0):
    rows = sh.shape[0]
    rw = pw["w0"].shape[1]
    pairs = rw // LANES
    c = RWKV_CHUNK
    n_all = rows // c
    nb = tt_chunk // c
    gcols = rw // LANES
    n_steps = (rows // tt_chunk) * pairs
    dbatch, n_pages = page_table.shape
    n_phys, page = cache_k.shape[0], cache_k.shape[1]
    assert (dbatch * n_pages) % n_steps == 0, "decode pages must spread evenly over the chunk grid"
    pps = dbatch * n_pages // n_steps
    assert n_pages % pps == 0
    dec = DecodePlan(pps=pps, sps=n_pages // pps, n_pages=n_pages, page=page, lam0=lam0)
    ck = cache_k.reshape(n_phys, page * A_HEADS, LANES)
    cv = cache_v.reshape(n_phys, page * A_HEADS, LANES)
    nrow = 2 * A_HEADS

    tile = pl.BlockSpec((tt_chunk, LANES), lambda i, p, *_: (i, p))
    mat = pl.BlockSpec((nb, None, LANES, LANES), lambda i, p, *_: (i, p, 0, 0))
    win = lambda g: pl.BlockSpec((tt_chunk, LANES), lambda i, p, *_: (i, g * gcols + p))
    lora = pl.BlockSpec((tt_chunk, LANES), lambda i, p, *_: (i, 3 * gcols))
    trow = lambda i: jnp.maximum(i * (tt_chunk // 8) - 1, 0)
    twin = lambda g: pl.BlockSpec((8, LANES), lambda i, p, *_: (trow(i), g * gcols + p))
    tlora = pl.BlockSpec((8, LANES), lambda i, p, *_: (trow(i), 3 * gcols))
    mwin = lambda g: pl.BlockSpec((1, LANES), lambda i, p, *_: (0, g * gcols + p))
    mlora = pl.BlockSpec((1, LANES), lambda i, p, *_: (0, 3 * gcols))
    vec = pl.BlockSpec((1, LANES), lambda i, p, *_: (0, p))
    up = pl.BlockSpec((LANES, LANES), lambda i, p, *_: (0, p))
    per_seq = lambda r: pl.BlockSpec((None, r, LANES), lambda i, p, *_: ((i * pairs + p) // dec.sps, 0, 0))
    cvec = lambda n: pl.BlockSpec((1, n), lambda i, p, *_: (0, 0))
    hbm = pl.BlockSpec(memory_space=pl.ANY)
    grid_spec = pltpu.PrefetchScalarGridSpec(
        num_scalar_prefetch=1, grid=(rows // tt_chunk, pairs),
        in_specs=[win(0), win(1), win(2), lora, twin(0), twin(1), twin(2), tlora,
                  mwin(0), mwin(1), mwin(2), mlora, vec, up, vec, up, vec, vec, vec,
                  per_seq(nrow), pl.BlockSpec((nrow, 1), lambda i, p, *_: (0, 0)),
                  per_seq(A_HEADS), per_seq(A_HEADS),
                  cvec(A_QK), cvec(A_QK), cvec(A_QK), cvec(A_QK), cvec(A_V), hbm, hbm],
        out_specs=[mat, mat, pl.BlockSpec((nb, None, 2 * c, LANES), lambda i, p, *_: (i, p, 0, 0)), tile,
                   pl.BlockSpec((nb, None, 1, LANES), lambda i, p, *_: (i, p, 0, 0)), tile,
                   per_seq(A_HEADS)],
        scratch_shapes=[pltpu.VMEM((2, pps, page * A_HEADS, LANES), F32),
                        pltpu.VMEM((2, pps, page * A_HEADS, LANES), F32),
                        pltpu.SemaphoreType.DMA((2, 2, pps)),
                        pltpu.VMEM((nrow, 1), F32), pltpu.VMEM((nrow, 1), F32),
                        pltpu.VMEM((nrow, LANES), F32)])
    m_, n_, lr_, op_, wc_, bonus, o_dec = pl.pallas_call(
        functools.partial(_rwkv_chunk_kernel, n_chunks=nb, tiles_per_seq=seq_len // tt_chunk,
                          pairs=pairs, dec=dec),
        out_shape=[jax.ShapeDtypeStruct((n_all, pairs, LANES, LANES), BF16),
                   jax.ShapeDtypeStruct((n_all, pairs, LANES, LANES), F32),
                   jax.ShapeDtypeStruct((n_all, pairs, 2 * c, LANES), BF16),
                   jax.ShapeDtypeStruct((rows, rw), F32),
                   jax.ShapeDtypeStruct((n_all, pairs, 1, LANES), F32),
                   jax.ShapeDtypeStruct((rows, rw), F32),
                   jax.ShapeDtypeStruct((dbatch, A_HEADS, LANES), F32)],
        grid_spec=grid_spec,
        compiler_params=_params("arbitrary", "arbitrary"),
        name="rwkv_chunk_decode",
    )(page_table, sh, sh, sh, sh, sh, sh, sh, sh, pw["mu"], pw["mu"], pw["mu"], pw["mu"],
      pw["w0"], pw["w2p"], pw["a0"], pw["a2p"], pw["k_k"], pw["k_a"], pw["r_k"],
      qm, aw["slopes16"], k_self, v_self, aw["lq1"], aw["lk1"], aw["lq2"], aw["lk2"], aw["subln_g"],
      ck, cv)

    nt = seq_len // tt_state
    ns = tt_state // c
    smat = lambda rws: pl.BlockSpec((ns, pairs, rws, LANES), lambda b, i: (b * nt + i, 0, 0, 0))
    wide = pl.BlockSpec((tt_state, rw), lambda b, i: (b * nt + i, 0))
    o, s = pl.pallas_call(
        functools.partial(_rwkv_state_kernel, n_chunks=ns, pairs=pairs),
        out_shape=[jax.ShapeDtypeStruct((rows, rw), F32),
                   jax.ShapeDtypeStruct((batch, pairs, LANES, LANES), F32)],
        grid=(batch, nt),
        in_specs=[smat(LANES), smat(LANES), smat(2 * c), wide, smat(1)],
        out_specs=[wide, pl.BlockSpec((None, pairs, LANES, LANES), lambda b, i: (b, 0, 0, 0))],
        scratch_shapes=[pltpu.VMEM((pairs, LANES, LANES), F32)],
        compiler_params=_params("parallel", "arbitrary"),
        name="rwkv_state",
    )(m_, n_, lr_, op_, wc_)
    s = s.reshape(batch, pairs, 2, R_HEAD, 2, R_HEAD)
    s = jnp.stack([s[:, :, 0, :, 0, :], s[:, :, 1, :, 1, :]], axis=2)
    return o, bonus, s.reshape(batch, 2 * pairs, R_HEAD, R_HEAD), o_dec


def _rwkv_step_kernel(s_ref, r_ref, lw_ref, k_ref, al_ref, be_ref, vcol_ref, o_ref, so_ref):
    s = s_ref[...]
    sa = jnp.sum(s * al_ref[...], axis=-1, keepdims=True)
    s_new = s * jnp.exp(lw_ref[...]) + sa * be_ref[...] + vcol_ref[...] * k_ref[...]
    so_ref[...] = s_new
    o_ref[...] = jnp.sum(s_new * r_ref[...], axis=-1, keepdims=True)


def _rwkv_step(state, r, lw, k, v, al, be):
    b, h, n, _ = state.shape
    heads = b * h
    hb = _tile(heads, 64)
    vec = lambda x: x.reshape(heads, 1, n)
    vspec = pl.BlockSpec((hb, 1, n), lambda i: (i, 0, 0))
    cspec = pl.BlockSpec((hb, n, 1), lambda i: (i, 0, 0))
    sspec = pl.BlockSpec((hb, n, n), lambda i: (i, 0, 0))
    o, s_new = pl.pallas_call(
        _rwkv_step_kernel,
        out_shape=[jax.ShapeDtypeStruct((heads, n, 1), F32), jax.ShapeDtypeStruct((heads, n, n), F32)],
        grid=(heads // hb,),
        in_specs=[sspec, vspec, vspec, vspec, vspec, vspec, cspec],
        out_specs=[cspec, sspec],
        compiler_params=_params("parallel"), name="rwkv_step",
    )(state.reshape(heads, n, n), vec(r), vec(lw), vec(k), vec(al), vec(be), v.reshape(heads, n, 1))
    return o.reshape(b, h * n), s_new.reshape(state.shape)


def _lambda(lq1, lk1, lq2, lk2, lam0):
    t1 = jnp.sum(lq1 * lk1, axis=-1, keepdims=True)
    t2 = jnp.sum(lq2 * lk2, axis=-1, keepdims=True)
    return jnp.exp(t1) - jnp.exp(t2) + lam0


def _diff_finish(o1, o2, lam, subln_g, lam0):
    o = o1 - lam * o2
    y = o * lax.rsqrt(jnp.mean(o * o, axis=-1, keepdims=True) + RMS_EPS)
    return y * subln_g * (1.0 - lam0)


ATTN_GROUP = 256
SUM_ROWS = 16
ATTN_KEYS = 1024
ATTN_AHEAD = 4
ALIBI_SPLIT = 256


def _attn_prompt_kernel(slopes_ref, q_ref, k_ref, v_ref, lq1, lk1, lq2, lk2, g_ref, o_ref,
                        kb, vt, *, tq, lam0):
    h = pl.program_id(1)
    qi = pl.program_id(2)
    seq_len = k_ref.shape[0]

    @pl.when(qi == 0)
    def _():
        slope = slopes_ref[h]
        pos = lax.broadcasted_iota(jnp.int32, (seq_len, LANES), 0)
        lane = lax.broadcasted_iota(jnp.int32, (seq_len, LANES), 1)
        lo = (pos % ALIBI_SPLIT).astype(F32) * slope
        hi = (pos - pos % ALIBI_SPLIT).astype(F32) * slope
        kb[:, :LANES] = k_ref[...].astype(BF16)
        n_alibi = 2 * len(LOG2E_TERMS)
        kb[:, LANES:] = jnp.where(lane >= n_alibi, 0.0, jnp.where(lane % 2 == 0, lo, hi)).astype(BF16)
        for c0 in range(0, seq_len, tq):
            vt[:LANES, c0:c0 + tq] = v_ref[c0:c0 + tq, :].T.astype(BF16)
        vt[LANES:, :] = jnp.ones((SUM_ROWS, seq_len), BF16)

    lane = lax.broadcasted_iota(jnp.int32, (tq, LANES), 1)
    q = q_ref[...]
    ones = jnp.zeros((tq, LANES), F32)
    for i, term in enumerate(LOG2E_TERMS):
        ones = jnp.where(lane // 2 == i, term, ones)
    ones = ones.astype(BF16)
    zero = jnp.zeros_like(q)
    q_aug = jnp.concatenate([
        jnp.concatenate([jnp.where(lane < A_QK, q, zero), ones], axis=1),
        jnp.concatenate([jnp.where(lane >= A_QK, q, zero), ones], axis=1)], axis=0)
    gw = ATTN_GROUP
    groups = range(2 * tq // gw)
    q_grp = [q_aug[g * gw:(g + 1) * gw, :] for g in groups]

    ksub = min(ATTN_KEYS, tq)

    def block(j, carry, diagonal):
        m, acc = (list(x) for x in carry)
        steps = []
        for a in range(tq // ksub):
            for g in groups:
                q0 = (g * gw) % tq
                rows = min(ksub, q0 + gw - a * ksub) if diagonal else ksub
                if rows > 0:
                    steps.append((a, g, rows, diagonal and (a * ksub + rows - 1 > q0)))

        def scores(a, g, rows, masked):
            ks = pl.ds(pl.multiple_of(j * tq + a * ksub, ksub), rows)
            s = _dg(kb[ks, :], q_grp[g], NT_DIMS)
            if masked:
                r_i = lax.broadcasted_iota(jnp.int32, (rows, gw), 0) + a * ksub
                c_i = lax.broadcasted_iota(jnp.int32, (rows, gw), 1) + (g * gw) % tq
                s = jnp.where(r_i <= c_i, s, NEG_INF)
            return s

        ahead = [scores(*st) for st in steps[:ATTN_AHEAD]]
        for idx, (a, g, rows, _) in enumerate(steps):
            if idx + ATTN_AHEAD < len(steps):
                ahead.append(scores(*steps[idx + ATTN_AHEAD]))
            s = ahead.pop(0)
            ks = pl.ds(pl.multiple_of(j * tq + a * ksub, ksub), rows)
            m_new = jnp.maximum(m[g], jnp.max(s, axis=0, keepdims=True))
            p = jnp.exp2(s - m_new)
            corr = jnp.exp2(m[g] - m_new)
            acc[g] = acc[g] * corr + _dot(vt[:, ks], p.astype(BF16))
            m[g] = m_new
        return tuple(m), tuple(acc)

    init = (tuple(jnp.full((1, gw), NEG_INF, F32) for _ in groups),
            tuple(jnp.zeros((LANES + SUM_ROWS, gw), F32) for _ in groups))
    carry = lax.fori_loop(0, qi, lambda j, cr: block(j, cr, False), init)
    _, acc = block(qi, carry, True)

    o = jnp.concatenate([a[:LANES] / a[LANES:LANES + 1] for a in acc], axis=1).T
    lam = _lambda(lq1[...], lk1[...], lq2[...], lk2[...], lam0)
    o_ref[...] = _diff_finish(o[:tq], o[tq:], lam, g_ref[...], lam0)


def _attn_prompt(q, k, v, aw, batch, seq_len, tq, lam0):
    rows, width = k.shape
    nq = seq_len // tq
    qspec = pl.BlockSpec((tq, LANES), lambda b, h, i, *_: (b * nq + i, h))
    kvspec = pl.BlockSpec((seq_len, LANES), lambda b, h, i, *_: (b, h))
    vec = lambda n: pl.BlockSpec((1, n), lambda b, h, i, *_: (0, 0))
    grid_spec = pltpu.PrefetchScalarGridSpec(
        num_scalar_prefetch=1, grid=(batch, A_HEADS, nq),
        in_specs=[qspec, kvspec, kvspec, vec(A_QK), vec(A_QK), vec(A_QK), vec(A_QK), vec(A_V)],
        out_specs=qspec,
        scratch_shapes=[pltpu.VMEM((seq_len, 2 * LANES), BF16), pltpu.VMEM((LANES + SUM_ROWS, seq_len), BF16)])
    return pl.pallas_call(
        functools.partial(_attn_prompt_kernel, tq=tq, lam0=lam0),
        out_shape=jax.ShapeDtypeStruct((rows, width), F32),
        grid_spec=grid_spec,
        compiler_params=_params("parallel", "parallel", "arbitrary"),
        name="attn_prompt",
    )(aw["slopes"], q, k, v, aw["lq1"], aw["lk1"], aw["lq2"], aw["lk2"], aw["subln_g"])


def _merge_kernel(or_ref, bo_ref, zr_ref, oa_ref, za_ref, ga_ref, gr_ref,
                  lng_ref, lnb_ref, wr_ref, wa_ref, o_ref):
    o_r = or_ref[...]
    inv_n = 1.0 / R_HEAD
    mean = _head_sums(o_r) * inv_n
    d = o_r - mean
    var = _head_sums(d * d) * inv_n
    o_r = d * lax.rsqrt(var + GN_EPS) * lng_ref[...] + lnb_ref[...] + bo_ref[...]
    f32 = lambda ref: ref[...].astype(F32)
    x_r = (o_r * _silu(f32(zr_ref))).astype(BF16)
    x_a = (oa_ref[...] * _silu(f32(za_ref))).astype(BF16)
    y_r = _dot(x_r, wr_ref[...])
    y_a = _dot(x_a, wa_ref[...])
    o_ref[...] = (_sigmoid(f32(ga_ref)) * y_a + _sigmoid(f32(gr_ref)) * y_r).astype(o_ref.dtype)


def _merge(o_r, bonus, o_a, zr, zag, mw, tm):
    rows, rw = o_r.shape
    d = mw["w_r_up"].shape[1]
    tile = pl.BlockSpec((tm, rw), lambda i: (i, 0))
    gate = lambda col0: pl.BlockSpec((pl.Element(tm), pl.Element(d)),
                                     lambda i: (pl.multiple_of(i * tm, 8), col0))
    const = lambda shape: pl.BlockSpec(shape, lambda i: (0,) * len(shape))
    return pl.pallas_call(
        _merge_kernel,
        out_shape=jax.ShapeDtypeStruct((rows, d), BF16),
        grid=(rows // tm,),
        in_specs=[tile, tile, tile, tile, tile, gate(rw), gate(rw + d),
                  const((1, rw)), const((1, rw)),
                  const(mw["w_r_up"].shape), const(mw["w_a_up"].shape)],
        out_specs=pl.BlockSpec((tm, d), lambda i: (i, 0)),
        compiler_params=_params("parallel"), name="merge",
    )(o_r, bonus, zr, o_a, zag, zag, zag, mw["lnx_g"], mw["lnx_b"], mw["w_r_up"], mw["w_a_up"])


def _out_kernel(x_ref, m_ref, p_ref, wo_ref, wp_ref, wg_ref, g_ref, o_ref):
    h = x_ref[...] + _dot(m_ref[...], wo_ref[...])
    ple = _dot(p_ref[...].astype(BF16), wp_ref[...])
    h = h + ple * _sigmoid(_dot(h.astype(BF16), wg_ref[...]))
    y = h * lax.rsqrt(jnp.mean(h * h, axis=-1, keepdims=True) + RMS_EPS)
    o_ref[...] = y * g_ref[...]


def _out_stage(x, m, p, ow, tm):
    rows, d = x.shape
    pd = p.shape[1]
    const = lambda shape: pl.BlockSpec(shape, lambda i: (0,) * len(shape),
                                       pipeline_mode=pl.Buffered(1))
    return pl.pallas_call(
        _out_kernel,
        out_shape=jax.ShapeDtypeStruct((rows, d), F32),
        grid=(rows // tm,),
        in_specs=[pl.BlockSpec((tm, d), lambda i: (i, 0)), pl.BlockSpec((tm, d), lambda i: (i, 0)),
                  pl.BlockSpec((tm, pd), lambda i: (i, 0)),
                  const((d, d)), const((pd, d)), const((d, d)), const((1, d))],
        out_specs=pl.BlockSpec((tm, d), lambda i: (i, 0)),
        compiler_params=_params("parallel"), name="out_stage",
    )(x, m, p, ow["w_out"], ow["w_ple"], ow["w_ple_gate"], ow["norm_final_g"])


def _tile(n, pref):
    return pref if n % pref == 0 else n


def kernel(x_prompt, x_sample, cache_k, cache_v, state_wkv, state_shift, page_table, p_prompt,
           p_sample, norm_in_g, w_in, mu_shift, w0, w2, a0, a2, k_k, k_a, r_k, lnx_g, lnx_b,
           w_rwkv_up, lambda_q1, lambda_k1, lambda_q2, lambda_k2, subln_g, w_attn_up, w_out,
           w_ple, w_ple_gate, norm_final_g):
    depth = w_in.shape[0]
    assert depth == 1, "single-layer trunk"
    layer = 0
    lam0 = 0.8 - 0.6 * math.exp(-0.3 * layer)
    bp, tp, d = x_prompt.shape
    bs, ts, _ = x_sample.shape
    assert ts == 1
    rw = w0.shape[1]
    sw = mu_shift.shape[1]
    aqk = A_HEADS * 2 * A_QK
    aw_ = A_HEADS * A_V
    row = lambda x: x.reshape(1, -1)

    w_all = w_in[layer].astype(BF16)
    c_zr, c_q, c_k, c_v, c_za = sw, sw + rw, sw + rw + aqk, sw + rw + 2 * aqk, sw + rw + 2 * aqk + aw_
    n_zag = w_all.shape[1] - c_za

    lora = w2.shape[1]
    zpad = jnp.zeros((LANES - lora, rw), BF16)
    pw = dict(mu=row(mu_shift[layer]), w0=row(w0[layer]), a0=row(a0[layer]), k_k=row(k_k[layer]),
              k_a=row(k_a[layer]), r_k=row(r_k[layer]),
              w2p=jnp.concatenate([w2[layer].astype(BF16), zpad], axis=0),
              a2p=jnp.concatenate([zpad, a2[layer].astype(BF16)], axis=0))
    slopes = 2.0 ** (-8.0 * jnp.arange(1, A_HEADS + 1, dtype=F32) / A_HEADS)
    aw = dict(slopes=slopes, slopes16=jnp.tile(slopes, 2).reshape(2 * A_HEADS, 1),
              lq1=row(lambda_q1[layer]), lk1=row(lambda_k1[layer]), lq2=row(lambda_q2[layer]),
              lk2=row(lambda_k2[layer]), subln_g=row(subln_g[layer]))
    mw = dict(lnx_g=row(lnx_g[layer]), lnx_b=row(lnx_b[layer]),
              w_r_up=w_rwkv_up[layer].astype(BF16), w_a_up=w_attn_up[layer].astype(BF16))
    ow = dict(w_out=w_out[layer].astype(BF16), w_ple=w_ple[layer].astype(BF16),
              w_ple_gate=w_ple_gate[layer].astype(BF16), norm_final_g=row(norm_final_g))

    def project(x2d, tm):
        xn, sh = _rms_proj(x2d, norm_in_g[layer], w_all, sw, min(tm, 512))
        zr = _matmul(xn, w_all, c_zr, rw, BF16, tm, 1024, name="proj_zr")
        q = _matmul(xn, w_all, c_q, aqk, BF16, tm, 1024, scale=A_QK ** -0.5 * LOG2E, name="proj_q")
        k = _matmul(xn, w_all, c_k, aqk, F32, tm, 1024, name="proj_k")
        v = _matmul(xn, w_all, c_v, aw_, F32, tm, 1024, name="proj_v")
        zag = _matmul(xn, w_all, c_za, n_zag, BF16, tm, 1024, name="proj_gates")
        return sh, zr, q, k, v, zag

    xp = x_prompt.reshape(bp * tp, d)
    xs = x_sample.reshape(bs, d)
    sh, zr, q, k, v, zag = project(xp, _tile(bp * tp, 1024))
    sh_s, zr_s, q_s, k_s, v_s, zag_s = project(xs, bs)
    q_h = q_s.reshape(bs, 1, A_HEADS, 2, A_QK)
    sel = jnp.eye(2, dtype=q_s.dtype).reshape(1, 2, 1, 2, 1)
    qm = (q_h * sel).reshape(bs, 2 * A_HEADS, 2 * A_QK)

    o_r, bonus, wkv_p, o_dec = _rwkv_scan(
        sh, pw, bp, tp, _tile(tp, 512), _tile(tp, 512), qm, k_s.reshape(bs, A_HEADS, 2 * A_QK),
        v_s.reshape(bs, A_HEADS, A_V), cache_k[layer], cache_v[layer], page_table, aw, lam0)
    o_a = _attn_prompt(q, k, v, aw, bp, tp, _tile(tp, 2048), lam0)
    m = _merge(o_r, bonus, o_a, zr, zag, mw, _tile(bp * tp, 256))
    y_p = _out_stage(xp, m, p_prompt[layer].reshape(bp * tp, -1), ow, _tile(bp * tp, 512))
    y_prompt = y_p.reshape(bp, tp, d)
    k_prompt = k.reshape(1, bp, tp, A_HEADS, 2 * A_QK)
    v_prompt = v.reshape(1, bp, tp, A_HEADS, A_V)
    shift_prompt = sh.reshape(bp, tp, sw)[:, -1][None]

    r_, lw_, k_, v_, al_, be_, bonus_s = _rwkv_prep_sample(sh_s, state_shift[layer], pw)
    o_r, wkv_s = _rwkv_step(state_wkv[layer], r_, lw_, k_, v_, al_, be_)
    m = _merge(o_r, bonus_s, o_dec.reshape(bs, aw_), zr_s, zag_s, mw, bs)
    y_s = _out_stage(xs, m, p_sample[layer].reshape(bs, -1), ow, bs)
    y_sample = y_s.reshape(bs, ts, d)
    k_sample = k_s.reshape(1, bs, ts, A_HEADS, 2 * A_QK)
    v_sample = v_s.reshape(1, bs, ts, A_HEADS, A_V)

    return (y_prompt, y_sample, k_prompt, v_prompt, k_sample, v_sample,
            wkv_p[None], wkv_s[None], shift_prompt, sh_s[None])
```
